```python
import jax, jax.numpy as jnp
from jax import lax
import numpy as np

D_MODEL = 1024
BATCH = 1
SEQ = 16384
DEPTH = 2
DEC_BATCH = 128
DEC_SEQ = 1
PAST_LEN = 16384
PAGE_SIZE = 128

HEAD_DIM = 64
A_HEADS = 8
A_KV_HEADS = 2
MOBA_BLOCK = 256
MOBA_TOPK = 3
B_HEADS = 8
B_KV_HEADS = 2
C_HEADS = 16
C_Q_LORA = 512
C_KV_LORA = 256
C_NOPE = 64
C_ROPE = 32
C_V = 64
D_FF = 2816
CONV_W = 3
ROPE_THETA = 10000.0
EPS = 1e-6
Q_BLOCK = 128
N_EVEN = (DEPTH + 1) // 2
N_ODD = DEPTH // 2
A_QW = A_HEADS * HEAD_DIM
A_KVW = A_KV_HEADS * HEAD_DIM
B_QW = B_HEADS * HEAD_DIM
B_KVW = B_KV_HEADS * HEAD_DIM
IN_COLS = A_QW + 2 * A_KVW + B_QW + 2 * B_KVW + B_HEADS
MIX_W = A_QW + B_QW
F32 = jnp.float32

kernel_name = 'moba_fox_mla_convffn_decode_step'


def rmsnorm(x, g):
    xf = x.astype(F32)
    y = xf * lax.rsqrt(jnp.mean(xf * xf, axis=-1, keepdims=True) + EPS)
    return (y * g.astype(F32)).astype(x.dtype)


def rope(x, pos):
    d = x.shape[-1]
    half = d // 2
    inv = ROPE_THETA ** (-jnp.arange(half, dtype=F32) * (2.0 / d))
    ang = pos.astype(F32)[:, None] * inv[None, :]
    cos = jnp.cos(ang)[:, None, :]
    sin = jnp.sin(ang)[:, None, :]
    xf = x.astype(F32)
    x1, x2 = xf[..., :half], xf[..., half:]
    return jnp.concatenate([x1 * cos - x2 * sin, x2 * cos + x1 * sin], axis=-1).astype(x.dtype)


def gather_pages(pool, layer, page_table):
    g = pool[layer, page_table]
    return g.reshape((g.shape[0], g.shape[1] * g.shape[2]) + g.shape[3:])


def map_query_blocks(fn, q_pos, *q_arrays):
    t = q_pos.shape[0]
    nq = t // Q_BLOCK
    blocks = tuple(a.reshape((a.shape[0], nq, Q_BLOCK) + a.shape[2:]).swapaxes(0, 1) for a in q_arrays)
    out = lax.map(lambda args: fn(*args), (q_pos.reshape(nq, Q_BLOCK),) + blocks)
    out = out.swapaxes(0, 1)
    return out.reshape((out.shape[0], t) + out.shape[3:])


def even_proj(h, pos, w_in, b_f):
    b, t, _ = h.shape
    z = h @ w_in
    cuts = np.cumsum([A_QW, A_KVW, A_KVW, B_QW, B_KVW, B_KVW]).tolist()
    qa, ka, va, qb, kb, vb, fl = jnp.split(z, cuts, axis=-1)
    heads = lambda a, n: a.reshape(b, t, n, HEAD_DIM)
    qa = rope(heads(qa, A_HEADS), pos)
    ka = rope(heads(ka, A_KV_HEADS), pos)
    va = heads(va, A_KV_HEADS)
    qb = heads(qb, B_HEADS)
    kb = heads(kb, B_KV_HEADS)
    vb = heads(vb, B_KV_HEADS)
    logf = jax.nn.log_sigmoid((fl + b_f).astype(F32))
    return qa, ka, va, qb, kb, vb, logf


def moba_blocks(k, v):
    b, l, hkv, d = k.shape
    nb = -(-l // MOBA_BLOCK)
    pad = nb * MOBA_BLOCK - l
    def blk(a):
        a = jnp.pad(a, ((0, 0), (0, pad), (0, 0), (0, 0)))
        return a.reshape(b, nb, MOBA_BLOCK, hkv, d).transpose(0, 3, 1, 2, 4)
    kb, vb = blk(k), blk(v)
    km = jnp.mean(kb.astype(F32), axis=3)
    return kb, vb, km


def moba_attend(q, k_blk, v_blk, k_mean, q_pos):
    b, t, h, d = q.shape
    hkv, nb = k_blk.shape[1], k_blk.shape[2]
    g = h // hkv
    scale = d ** -0.5
    qf = q.astype(F32)
    qg = qf.reshape(b, t, hkv, g, d)
    own = q_pos // MOBA_BLOCK
    gate = jnp.einsum('btkgd,bknd->btkgn', qg, k_mean).reshape(b, t, h, nb)
    fully_past = jnp.arange(nb)[None, :] < own[:, None]
    gate = jnp.where(fully_past[None, :, None, :], gate, -jnp.inf)
    n_cand = max(nb, MOBA_TOPK)
    gate = jnp.pad(gate, ((0, 0), (0, 0), (0, 0), (0, n_cand - nb)), constant_values=-jnp.inf)
    top_val, top_idx = lax.top_k(gate, MOBA_TOPK)
    sel_ok = jnp.isfinite(top_val)
    top_idx = jnp.minimum(top_idx, nb - 1)
    bi = jnp.arange(b)[:, None, None, None]
    hi = (jnp.arange(h) // g)[None, None, :, None]
    k_sel = k_blk[bi, hi, top_idx].astype(F32)
    v_sel = v_blk[bi, hi, top_idx].astype(F32)
    s_sel = jnp.einsum('bthd,bthjsd->bthjs', qf, k_sel) * scale
    s_sel = jnp.where(sel_ok[..., None], s_sel, -jnp.inf).reshape(b, t, h, MOBA_TOPK * MOBA_BLOCK)
    k_own = k_blk[:, :, own].astype(F32)
    v_own = v_blk[:, :, own].astype(F32)
    s_own = jnp.einsum('btkgd,bktsd->btkgs', qg, k_own).reshape(b, t, h, MOBA_BLOCK) * scale
    own_pos = own[:, None] * MOBA_BLOCK + jnp.arange(MOBA_BLOCK)[None, :]
    s_own = jnp.where((own_pos <= q_pos[:, None])[None, :, None, :], s_own, -jnp.inf)
    p = jax.nn.softmax(jnp.concatenate([s_sel, s_own], axis=-1), axis=-1)
    p_sel = p[..., :MOBA_TOPK * MOBA_BLOCK].reshape(b, t, h, MOBA_TOPK, MOBA_BLOCK)
    p_own = p[..., MOBA_TOPK * MOBA_BLOCK:].reshape(b, t, hkv, g, MOBA_BLOCK)
    o = jnp.einsum('bthjs,bthjsd->bthd', p_sel, v_sel)
    o = o + jnp.einsum('btkgs,bktsd->btkgd', p_own, v_own).reshape(b, t, h, d)
    return o.astype(q.dtype)


def moba_prompt(q, k, v, pos):
    kb, vb, km = moba_blocks(k, v)
    return map_query_blocks(lambda p, qq: moba_attend(qq, kb, vb, km, p), pos, q)


def fox_attend(q, k, v, cum_q, cum_k, q_pos, k_pos):
    b, t, h, d = q.shape
    l, hkv = k.shape[1], k.shape[2]
    g = h // hkv
    qg = q.astype(F32).reshape(b, t, hkv, g, d)
    s = jnp.einsum('btkgd,bskd->bkgts', qg, k.astype(F32)) * (d ** -0.5)
    decay = cum_q.transpose(0, 2, 1)[..., :, None] - cum_k.transpose(0, 2, 1)[..., None, :]
    s = s + decay.reshape(b, hkv, g, t, l)
    s = jnp.where(k_pos[None, :] <= q_pos[:, None], s, -jnp.inf)
    p = jax.nn.softmax(s, axis=-1)
    o = jnp.einsum('bkgts,bskd->btkgd', p, v.astype(F32))
    return o.reshape(b, t, h, d).astype(q.dtype)


def fox_prompt(q, k, v, logf, pos):
    cum = jnp.cumsum(logf, axis=1)
    return map_query_blocks(lambda p, qq, cq: fox_attend(qq, k, v, cq, cum, p, pos), pos, q, cum)


def mla_proj(h, pos, w_dq, g_q, w_uq, w_dkv, g_kv):
    b, t, _ = h.shape
    cq = rmsnorm(h @ w_dq, g_q)
    q = (cq @ w_uq).reshape(b, t, C_HEADS, C_NOPE + C_ROPE)
    q_nope, q_pe = q[..., :C_NOPE], rope(q[..., C_NOPE:], pos)
    kv = h @ w_dkv
    ckv = rmsnorm(kv[..., :C_KV_LORA], g_kv)
    kpe = rope(kv[..., None, C_KV_LORA:], pos)[:, :, 0]
    return q_nope, q_pe, ckv, kpe


def mla_prompt(q_nope, q_pe, ckv, kpe, pos, w_uk, w_uv):
    k_nope = jnp.einsum('bsc,chd->bshd', ckv, w_uk).astype(F32)
    v = jnp.einsum('bsc,chd->bshd', ckv, w_uv).astype(F32)
    kpe_f = kpe.astype(F32)
    scale = (C_NOPE + C_ROPE) ** -0.5
    def block(p, qn, qp):
        s = (jnp.einsum('bthd,bshd->bhts', qn.astype(F32), k_nope)
             + jnp.einsum('bthr,bsr->bhts', qp.astype(F32), kpe_f)) * scale
        s = jnp.where(pos[None, :] <= p[:, None], s, -jnp.inf)
        w = jax.nn.softmax(s, axis=-1)
        return jnp.einsum('bhts,bshd->bthd', w, v).astype(qn.dtype)
    return map_query_blocks(block, pos, q_nope, q_pe)


def mla_sample(q_nope, q_pe, ckv_all, kpe_all, q_pos, k_pos, w_uk, w_uv):
    scale = (C_NOPE + C_ROPE) ** -0.5
    c = ckv_all.astype(F32)
    q_lat = jnp.einsum('bthd,chd->bthc', q_nope.astype(F32), w_uk.astype(F32))
    s = (jnp.einsum('bthc,bsc->bhts', q_lat, c)
         + jnp.einsum('bthr,bsr->bhts', q_pe.astype(F32), kpe_all.astype(F32))) * scale
    s = jnp.where(k_pos[None, :] <= q_pos[:, None], s, -jnp.inf)
    w = jax.nn.softmax(s, axis=-1)
    o_lat = jnp.einsum('bhts,bsc->bthc', w, c)
    return jnp.einsum('bthc,chd->bthd', o_lat, w_uv.astype(F32)).astype(q_nope.dtype)


def conv_ffn(h, conv_state, w_gate, w_up, conv_w, conv_b, w_down):
    t = h.shape[1]
    g = h @ w_gate
    u = h @ w_up
    gp = jnp.concatenate([conv_state.astype(g.dtype), g], axis=1)
    gc = conv_b + sum(conv_w[j] * gp[:, j:j + t] for j in range(CONV_W))
    y = (jax.nn.silu(gc) * u) @ w_down
    return y, gp[:, t:]


def setup_inputs(seed: int = 0) -> dict:
    key = jax.random.key(seed)
    ks = jax.random.split(key, 40)
    n_pages = PAST_LEN // PAGE_SIZE
    n_pool = (5 * DEC_BATCH * n_pages) // 4
    nrm = lambda k, shape, scale=1.0: scale * jax.random.normal(k, shape, F32)
    page_table = jax.random.permutation(ks[0], n_pool)[:DEC_BATCH * n_pages].reshape(DEC_BATCH, n_pages).astype(jnp.int32)
    fbias = jnp.linspace(1.0, 5.0, B_HEADS, dtype=F32)
    return {
        'x_prompt': nrm(ks[1], (BATCH, SEQ, D_MODEL)),
        'x_sample': nrm(ks[2], (DEC_BATCH, DEC_SEQ, D_MODEL)),
        'cache_moba_k': nrm(ks[3], (N_EVEN, n_pool, PAGE_SIZE, A_KV_HEADS, HEAD_DIM)),
        'cache_moba_v': nrm(ks[4], (N_EVEN, n_pool, PAGE_SIZE, A_KV_HEADS, HEAD_DIM)),
        'cache_fox_k': nrm(ks[5], (N_EVEN, n_pool, PAGE_SIZE, B_KV_HEADS, HEAD_DIM)),
        'cache_fox_v': nrm(ks[6], (N_EVEN, n_pool, PAGE_SIZE, B_KV_HEADS, HEAD_DIM)),
        'cache_fox_logf': jax.nn.log_sigmoid(fbias + nrm(ks[7], (N_EVEN, n_pool, PAGE_SIZE, B_HEADS))),
        'cache_mla_ckv': nrm(ks[8], (N_ODD, n_pool, PAGE_SIZE, C_KV_LORA)),
        'cache_mla_kpe': nrm(ks[9], (N_ODD, n_pool, PAGE_SIZE, C_ROPE)),
        'state_ffn_conv': nrm(ks[10], (DEPTH, DEC_BATCH, CONV_W - 1, D_FF)),
        'page_table': page_table,
        'norm_mix': 1.0 + nrm(ks[11], (DEPTH, D_MODEL), 0.05),
        'norm_ffn': 1.0 + nrm(ks[12], (DEPTH, D_MODEL), 0.05),
        'norm_final': 1.0 + nrm(ks[13], (D_MODEL,), 0.05),
        'w_in_even': nrm(ks[14], (N_EVEN, D_MODEL, IN_COLS), D_MODEL ** -0.5),
        'b_forget': fbias + nrm(ks[15], (N_EVEN, B_HEADS), 0.1),
        'w_out_even': nrm(ks[16], (N_EVEN, MIX_W, D_MODEL), MIX_W ** -0.5),
        'w_dq': nrm(ks[17], (N_ODD, D_MODEL, C_Q_LORA), D_MODEL ** -0.5),
        'g_q': 1.0 + nrm(ks[18], (N_ODD, C_Q_LORA), 0.05),
        'w_uq': nrm(ks[19], (N_ODD, C_Q_LORA, C_HEADS * (C_NOPE + C_ROPE)), C_Q_LORA ** -0.5),
        'w_dkv': nrm(ks[20], (N_ODD, D_MODEL, C_KV_LORA + C_ROPE), D_MODEL ** -0.5),
        'g_kv': 1.0 + nrm(ks[21], (N_ODD, C_KV_LORA), 0.05),
        'w_uk': nrm(ks[22], (N_ODD, C_KV_LORA, C_HEADS, C_NOPE), C_KV_LORA ** -0.5),
        'w_uv': nrm(ks[23], (N_ODD, C_KV_LORA, C_HEADS, C_V), C_KV_LORA ** -0.5),
        'w_out_odd': nrm(ks[24], (N_ODD, C_HEADS * C_V, D_MODEL), (C_HEADS * C_V) ** -0.5),
        'w_gate': nrm(ks[25], (DEPTH, D_MODEL, D_FF), D_MODEL ** -0.5),
        'w_up': nrm(ks[26], (DEPTH, D_MODEL, D_FF), D_MODEL ** -0.5),
        'conv_w': nrm(ks[27], (DEPTH, CONV_W, D_FF), CONV_W ** -0.5),
        'conv_b': nrm(ks[28], (DEPTH, D_FF), 0.02),
        'w_down': nrm(ks[29], (DEPTH, D_FF, D_MODEL), D_FF ** -0.5),
    }


def reference(x_prompt, x_sample, cache_moba_k, cache_moba_v, cache_fox_k, cache_fox_v, cache_fox_logf,
              cache_mla_ckv, cache_mla_kpe, state_ffn_conv, page_table,
              norm_mix, norm_ffn, norm_final, w_in_even, b_forget, w_out_even,
              w_dq, g_q, w_uq, w_dkv, g_kv, w_uk, w_uv, w_out_odd,
              w_gate, w_up, conv_w, conv_b, w_down):
    bp, sp, _ = x_prompt.shape
    bs, ts, _ = x_sample.shape
    past = page_table.shape[1] * PAGE_SIZE
    pos_p = jnp.arange(sp, dtype=jnp.int32)
    pos_s = past + jnp.arange(ts, dtype=jnp.int32)
    kpos_s = jnp.arange(past + ts, dtype=jnp.int32)
    xp, xs = x_prompt, x_sample
    mk_p, mv_p, fk_p, fv_p, fl_p, ck_p, kp_p, cv_p = [], [], [], [], [], [], [], []
    mk_s, mv_s, fk_s, fv_s, fl_s, ck_s, kp_s, cv_s = [], [], [], [], [], [], [], []
    for layer in range(DEPTH):
        if layer % 2 == 0:
            e = layer // 2
            qa, ka, va, qb, kb, vb, lf = even_proj(rmsnorm(xp, norm_mix[layer]), pos_p, w_in_even[e], b_forget[e])
            oa = moba_prompt(qa, ka, va, pos_p)
            ob = fox_prompt(qb, kb, vb, lf, pos_p)
            xp = xp + jnp.concatenate([oa.reshape(bp, sp, A_QW), ob.reshape(bp, sp, B_QW)], axis=-1) @ w_out_even[e]
            mk_p.append(ka); mv_p.append(va); fk_p.append(kb); fv_p.append(vb); fl_p.append(lf)
            qa, ka, va, qb, kb, vb, lf = even_proj(rmsnorm(xs, norm_mix[layer]), pos_s, w_in_even[e], b_forget[e])
            ka_all = jnp.concatenate([gather_pages(cache_moba_k, e, page_table), ka], axis=1)
            va_all = jnp.concatenate([gather_pages(cache_moba_v, e, page_table), va], axis=1)
            kbk, vbk, km = moba_blocks(ka_all, va_all)
            oa = moba_attend(qa, kbk, vbk, km, pos_s)
            kb_all = jnp.concatenate([gather_pages(cache_fox_k, e, page_table), kb], axis=1)
            vb_all = jnp.concatenate([gather_pages(cache_fox_v, e, page_table), vb], axis=1)
            cum = jnp.cumsum(jnp.concatenate([gather_pages(cache_fox_logf, e, page_table).astype(F32), lf], axis=1), axis=1)
            ob = fox_attend(qb, kb_all, vb_all, cum[:, past:], cum, pos_s, kpos_s)
            xs = xs + jnp.concatenate([oa.reshape(bs, ts, A_QW), ob.reshape(bs, ts, B_QW)], axis=-1) @ w_out_even[e]
            mk_s.append(ka); mv_s.append(va); fk_s.append(kb); fv_s.append(vb); fl_s.append(lf)
        else:
            o = layer // 2
            qn, qpe, ckv, kpe = mla_proj(rmsnorm(xp, norm_mix[layer]), pos_p, w_dq[o], g_q[o], w_uq[o], w_dkv[o], g_kv[o])
            oc = mla_prompt(qn, qpe, ckv, kpe, pos_p, w_uk[o], w_uv[o])
            xp = xp + oc.reshape(bp, sp, C_HEADS * C_V) @ w_out_odd[o]
            ck_p.append(ckv); kp_p.append(kpe)
            qn, qpe, ckv, kpe = mla_proj(rmsnorm(xs, norm_mix[layer]), pos_s, w_dq[o], g_q[o], w_uq[o], w_dkv[o], g_kv[o])
            ckv_all = jnp.concatenate([gather_pages(cache_mla_ckv, o, page_table), ckv], axis=1)
            kpe_all = jnp.concatenate([gather_pages(cache_mla_kpe, o, page_table), kpe], axis=1)
            oc = mla_sample(qn, qpe, ckv_all, kpe_all, pos_s, kpos_s, w_uk[o], w_uv[o])
            xs = xs + oc.reshape(bs, ts, C_HEADS * C_V) @ w_out_odd[o]
            ck_s.append(ckv); kp_s.append(kpe)
        f, st = conv_ffn(rmsnorm(xp, norm_ffn[layer]), jnp.zeros((bp, CONV_W - 1, D_FF), xp.dtype),
                         w_gate[layer], w_up[layer], conv_w[layer], conv_b[layer], w_down[layer])
        xp = xp + f
        cv_p.append(st)
        f, st = conv_ffn(rmsnorm(xs, norm_ffn[layer]), state_ffn_conv[layer],
                         w_gate[layer], w_up[layer], conv_w[layer], conv_b[layer], w_down[layer])
        xs = xs + f
        cv_s.append(st)
    return (rmsnorm(xp, norm_final), rmsnorm(xs, norm_final),
            jnp.stack(mk_p), jnp.stack(mv_p), jnp.stack(fk_p), jnp.stack(fv_p), jnp.stack(fl_p),
            jnp.stack(ck_p), jnp.stack(kp_p), jnp.stack(cv_p),
            jnp.stack(mk_s), jnp.stack(mv_s), jnp.stack(fk_s), jnp.stack(fv_s), jnp.stack(fl_s),
            jnp.stack(ck_s), jnp.stack(kp_s), jnp.stack(cv_s))
```

```python
import functools

import numpy as np
import jax
import jax.numpy as jnp
from jax import lax
from jax.experimental import pallas as pl
from jax.experimental.pallas import tpu as pltpu

F32 = jnp.float32
BF16 = jnp.bfloat16

HEAD_DIM = 64
A_HEADS, A_KV_HEADS = 8, 2
B_HEADS, B_KV_HEADS = 8, 2
C_HEADS, C_Q_LORA, C_KV_LORA, C_NOPE, C_ROPE, C_V = 16, 512, 256, 64, 32, 64
MOBA_BLOCK, MOBA_TOPK = 256, 3
PAGE_SIZE = 128
CONV_W = 3
ROPE_THETA = 10000.0
EPS = 1e-6
LANES = 128
NEG = -1e30
VMEM_LIMIT = 56 * 1024 * 1024
HIGHEST = lax.Precision.HIGHEST
NT_DIMS = (((1,), (1,)), ((), ()))
TN_DIMS = (((0,), (0,)), ((), ()))


def _params(*sem):
    return pltpu.CompilerParams(dimension_semantics=sem, vmem_limit_bytes=VMEM_LIMIT)


def _rms(x, g):
    return x * lax.rsqrt(jnp.mean(x * x, axis=-1, keepdims=True) + EPS) * g


def _rope_tables(pos, kind, freq, half, d):
    inv = ROPE_THETA ** (-jnp.arange(half, dtype=F32) * (2.0 / d))
    ang = pos.astype(F32)[:, None] * inv[None, :]
    cos, sin = jnp.cos(ang)[:, freq], jnp.sin(ang)[:, freq]
    kind = jnp.asarray(kind)[None, :]
    return (jnp.where(kind == 0, 1.0, cos).astype(F32), jnp.where(kind == 1, -sin, 0.0).astype(F32),
            jnp.where(kind == 2, sin, 0.0).astype(F32))


def _rope_apply(z, cos, sa, sb, half):
    outs = []
    for c in range(z.shape[1] // LANES):
        x = z[:, c * LANES:(c + 1) * LANES]
        outs.append(x * cos + pltpu.roll(x, LANES - half, 1) * sa + pltpu.roll(x, half, 1) * sb)
    return outs[0] if len(outs) == 1 else jnp.concatenate(outs, axis=1)


def _split3(x):
    hi = x.astype(BF16)
    r = x - hi.astype(F32)
    mid = r.astype(BF16)
    lo = (r - mid.astype(F32)).astype(BF16)
    return hi, mid, lo


def _row_tile(t, want):
    return want if t % want == 0 else t


def _even_proj_body(x_ref, g_ref, w_ref, bf_ref, cos_ref, sa_ref, sb_ref,
                    qa_ref, ka_ref, va_ref, qb_ref, kb_ref, vb_ref, lf_ref, chi_ref, cmid_ref, clo_ref, carry_ref):
    @pl.when(pl.program_id(0) == 0)
    def _():
        carry_ref[...] = jnp.zeros_like(carry_ref)

    tm = x_ref.shape[0]
    h = _rms(x_ref[...], g_ref[...]).astype(BF16)
    z = jnp.dot(h, w_ref[...], preferred_element_type=F32)
    cos, sa, sb = cos_ref[...], sa_ref[...], sb_ref[...]
    half = HEAD_DIM // 2
    scale = HEAD_DIM ** -0.5
    o = 0
    qa_ref[...] = (_rope_apply(z[:, o:o + 512], cos, sa, sb, half) * scale).astype(BF16); o += 512
    ka_ref[...] = _rope_apply(z[:, o:o + 128], cos, sa, sb, half); o += 128
    va_ref[...] = z[:, o:o + 128]; o += 128
    qb_ref[...] = (z[:, o:o + 512] * scale).astype(BF16); o += 512
    kb_ref[...] = z[:, o:o + 128]; o += 128
    vb_ref[...] = z[:, o:o + 128]; o += 128
    fl = z[:, o:o + 128] + bf_ref[...]
    lf = jnp.minimum(fl, 0.0) - jnp.log1p(jnp.exp(-jnp.abs(fl)))
    lf_ref[...] = lf[:, :B_HEADS]
    tri = (lax.broadcasted_iota(jnp.int32, (tm, tm), 0) >= lax.broadcasted_iota(jnp.int32, (tm, tm), 1)).astype(F32)
    cum = jnp.dot(tri, lf, preferred_element_type=F32, precision=HIGHEST) + carry_ref[...]
    carry_ref[...] = cum[tm - 1:tm, :]
    hi, mid, lo = _split3(cum[:, :B_HEADS])
    chi_ref[...] = hi
    cmid_ref[...] = mid
    clo_ref[...] = lo


def _even_proj(x, g, w_cat, bf_pad, pos):
    t, d = x.shape
    tm = _row_tile(t, 512)
    lane = np.arange(LANES) % HEAD_DIM
    cos, sa, sb = _rope_tables(pos, np.where(lane < 32, 1, 2), lane % 32, HEAD_DIM // 2, HEAD_DIM)
    row = lambda n: pl.BlockSpec((tm, n), lambda i: (i, 0))
    full = lambda a: pl.BlockSpec(a.shape, lambda i: (0, 0))
    outs = [(512, BF16), (128, F32), (128, F32), (512, BF16), (128, F32), (128, F32), (B_HEADS, F32),
            (B_HEADS, BF16), (B_HEADS, BF16), (B_HEADS, BF16)]
    return pl.pallas_call(
        _even_proj_body,
        grid=(t // tm,),
        in_specs=[row(d), full(g), full(w_cat), full(bf_pad), row(LANES), row(LANES), row(LANES)],
        out_specs=[row(n) for n, _ in outs],
        out_shape=[jax.ShapeDtypeStruct((t, n), dt) for n, dt in outs],
        scratch_shapes=[pltpu.VMEM((1, LANES), F32)],
        compiler_params=_params("arbitrary"),
        name="even_proj",
    )(x, g, w_cat, bf_pad, cos, sa, sb)


def _block_mean_body(k_ref, o_ref):
    nb = o_ref.shape[0]
    o_ref[...] = jnp.mean(k_ref[...].reshape(nb, MOBA_BLOCK, LANES), axis=1)


def _block_mean(k):
    t = k.shape[0]
    nb = t // MOBA_BLOCK
    return pl.pallas_call(
        _block_mean_body,
        out_shape=jax.ShapeDtypeStruct((nb, LANES), F32),
        compiler_params=_params(),
        name="moba_block_mean",
    )(k)


def _top3_bias(gate, n_valid, lane):
    cand = jnp.where(lane < n_valid, gate, -jnp.inf)
    keep = lane >= n_valid
    for _ in range(MOBA_TOPK):
        mx = jnp.max(cand, axis=-1, keepdims=True)
        first = jnp.min(jnp.where(cand == mx, lane, gate.shape[-1]), axis=-1, keepdims=True)
        hit = (lane == first) & (mx > -jnp.inf)
        keep = keep | hit
        cand = jnp.where(hit, -jnp.inf, cand)
    return jnp.where(keep, 0.0, NEG)


def _moba_gate_body(q_ref, km_ref, o_ref):
    tm = q_ref.shape[0]
    nb = km_ref.shape[0]
    g = A_HEADS // A_KV_HEADS
    pos = pl.program_id(0) * tm + lax.broadcasted_iota(jnp.int32, (tm, 1), 0)
    own = pos // MOBA_BLOCK
    lane = lax.broadcasted_iota(jnp.int32, (tm, nb), 1)
    for h in range(A_HEADS):
        q = q_ref[:, h * HEAD_DIM:(h + 1) * HEAD_DIM].astype(F32)
        km = km_ref[:, (h // g) * HEAD_DIM:(h // g + 1) * HEAD_DIM]
        gate = lax.dot_general(q, km, NT_DIMS, preferred_element_type=F32, precision=HIGHEST)
        bias = _top3_bias(gate, own, lane)
        if nb < HEAD_DIM:
            bias = jnp.concatenate([bias, jnp.zeros((tm, HEAD_DIM - nb), F32)], axis=1)
        o_ref[:, h * HEAD_DIM:(h + 1) * HEAD_DIM] = bias.astype(BF16)


def _moba_gate(q, kmean):
    t = q.shape[0]
    tm = _row_tile(t, 512)
    assert kmean.shape[0] <= HEAD_DIM, "one bias lane per key block"
    return pl.pallas_call(
        _moba_gate_body,
        grid=(t // tm,),
        in_specs=[pl.BlockSpec((tm, q.shape[1]), lambda i: (i, 0)), pl.BlockSpec(kmean.shape, lambda i: (0, 0))],
        out_specs=pl.BlockSpec((tm, A_HEADS * HEAD_DIM), lambda i: (i, 0)),
        out_shape=jax.ShapeDtypeStruct((t, A_HEADS * HEAD_DIM), BF16),
        compiler_params=_params("parallel"),
        name="moba_gate",
    )(q, kmean)


def _flash_body(q_ref, k_ref, vt_ref, o_ref, *, g, tq, tk):
    i = pl.program_id(1)
    r = g * tq
    n_full = (i * tq) // tk
    n_mask = tq // tk
    qs = [jnp.concatenate([q_ref[:, (s * g + a) * LANES:(s * g + a + 1) * LANES] for a in range(g)], axis=0)
          if g > 1 else q_ref[:, s * LANES:(s + 1) * LANES] for s in range(2)]

    def tile(j, carry, masked):
        start = pl.multiple_of(j * tk, tk)
        out = []
        for s in range(2):
            m, l, acc = carry[s]
            k = k_ref[s, pl.ds(start, tk), :]
            st = lax.dot_general(k, qs[s], NT_DIMS, preferred_element_type=F32)
            if masked:
                kpos = start + lax.broadcasted_iota(jnp.int32, (tk, r), 0)
                qpos = i * tq + lax.broadcasted_iota(jnp.int32, (tk, r), 1) % tq
                st = jnp.where(kpos <= qpos, st, NEG)
            m_new = jnp.maximum(m, jnp.max(st, axis=0, keepdims=True))
            p = jnp.exp(st - m_new)
            alpha = jnp.exp(m - m_new)
            l = alpha * l + jnp.sum(p, axis=0, keepdims=True)
            vt = vt_ref[s * HEAD_DIM:(s + 1) * HEAD_DIM, pl.ds(start, tk)]
            acc = alpha * acc + jnp.dot(vt, p.astype(BF16), preferred_element_type=F32)
            out.append((m_new, l, acc))
        return tuple(out)

    init = tuple((jnp.full((1, r), NEG, F32), jnp.zeros((1, r), F32), jnp.zeros((HEAD_DIM, r), F32)) for _ in range(2))
    carry = lax.fori_loop(0, n_full, lambda j, c: tile(j, c, False), init)
    for jm in range(n_mask):
        carry = tile(n_full + jm, carry, True)
    for s in range(2):
        _, l, acc = carry[s]
        o = acc / l
        for a in range(g):
            col = (s * g + a) * HEAD_DIM
            o_ref[:, col:col + HEAD_DIM] = o[:, a * tq:(a + 1) * tq].T.astype(BF16)


def _flash(qp, kp, vt, *, g, tq, tk):
    t = qp.shape[0]
    nc = kp.shape[0] // 2
    tq, tk = min(tq, t), min(tk, t)
    assert t % tq == 0 and tq % tk == 0
    return pl.pallas_call(
        functools.partial(_flash_body, g=g, tq=tq, tk=tk),
        grid=(nc, t // tq),
        in_specs=[pl.BlockSpec((tq, 2 * g * LANES), lambda c, i: (i, c)),
                  pl.BlockSpec((2, t, LANES), lambda c, i: (c, 0, 0)),
                  pl.BlockSpec((LANES, t), lambda c, i: (c, 0))],
        out_specs=pl.BlockSpec((tq, 2 * g * HEAD_DIM), lambda c, i: (i, c)),
        out_shape=jax.ShapeDtypeStruct((t, nc * 2 * g * HEAD_DIM), BF16),
        compiler_params=_params("parallel", "arbitrary"),
        name="flash_attention",
    )(qp, kp, vt)


def _out_proj_body(*refs, n):
    x_ref, o_ref = refs[0], refs[-1]
    acc = x_ref[...]
    for a_ref, w_ref in zip(refs[1:1 + n], refs[1 + n:1 + 2 * n]):
        acc = acc + jnp.dot(a_ref[...], w_ref[...], preferred_element_type=F32)
    o_ref[...] = acc


def _out_proj(x, acts, ws):
    t, d = x.shape
    tm = _row_tile(t, 512)
    return pl.pallas_call(
        functools.partial(_out_proj_body, n=len(acts)),
        grid=(t // tm,),
        in_specs=[pl.BlockSpec((tm, d), lambda i: (i, 0))]
                 + [pl.BlockSpec((tm, a.shape[1]), lambda i: (i, 0)) for a in acts]
                 + [pl.BlockSpec(w.shape, lambda i: (0, 0)) for w in ws],
        out_specs=pl.BlockSpec((tm, d), lambda i: (i, 0)),
        out_shape=jax.ShapeDtypeStruct((t, d), F32),
        compiler_params=_params("parallel"),
        name="out_proj",
    )(x, *acts, *ws)


FF_CHUNK = 256


def _ffn_body(*refs, seq):
    if seq:
        x_ref, gn_ref, wg_ref, wu_ref, cw_ref, cb_ref, wd_ref, y_ref, st_ref, carry_ref = refs
    else:
        x_ref, gn_ref, wg_ref, wu_ref, cw_ref, cb_ref, wd_ref, s0_ref, s1_ref, y_ref, st_ref = refs
    tm = x_ref.shape[0]
    ff = wg_ref.shape[1]
    if seq:
        @pl.when(pl.program_id(0) == 0)
        def _():
            carry_ref[...] = jnp.zeros_like(carry_ref)
        row = lax.broadcasted_iota(jnp.int32, (tm, FF_CHUNK), 0)

    x = x_ref[...]
    h = _rms(x, gn_ref[...]).astype(BF16)
    acc = x
    for c in range(ff // FF_CHUNK):
        sl = slice(c * FF_CHUNK, (c + 1) * FF_CHUNK)
        gch = jnp.dot(h, wg_ref[:, sl], preferred_element_type=F32)
        uch = jnp.dot(h, wu_ref[:, sl], preferred_element_type=F32)
        if seq:
            prev = carry_ref[:, sl]
            g1 = jnp.where(row == 0, prev[7:8, :], pltpu.roll(gch, 1, 0))
            g2 = jnp.where(row == 0, prev[6:7, :], jnp.where(row == 1, prev[7:8, :], pltpu.roll(gch, 2, 0)))
            carry_ref[:, sl] = gch[tm - 8:tm, :]
            st_ref[:, sl] = gch[tm - 8:tm, :]
        else:
            g1, g2 = s1_ref[:, sl], s0_ref[:, sl]
            st_ref[:, sl] = gch
        gc = cb_ref[:, sl] + cw_ref[0:1, sl] * g2 + cw_ref[1:2, sl] * g1 + cw_ref[2:3, sl] * gch
        act = gc / (1.0 + jnp.exp(-gc)) * uch
        acc = acc + jnp.dot(act.astype(BF16), wd_ref[sl, :], preferred_element_type=F32)
    y_ref[...] = acc


def _ffn(x, gn, wg, wu, cw, cb, wd, state=None):
    t, d = x.shape
    ff = wg.shape[1]
    assert ff % FF_CHUNK == 0
    seq = state is None
    tm = _row_tile(t, 512) if seq else t
    assert tm >= 8
    row = lambda n: pl.BlockSpec((tm, n), lambda i: (i, 0))
    full = lambda a: pl.BlockSpec(a.shape, lambda i: (0, 0))
    ins = [x, gn, wg, wu, cw, cb, wd]
    in_specs = [row(d)] + [full(a) for a in ins[1:]]
    if seq:
        st_shape, st_spec = (8, ff), pl.BlockSpec((8, ff), lambda i: (0, 0))
        scratch = [pltpu.VMEM((8, ff), F32)]
    else:
        ins += [state[:, 0], state[:, 1]]
        in_specs += [row(ff), row(ff)]
        st_shape, st_spec = (t, ff), row(ff)
        scratch = []
    return pl.pallas_call(
        functools.partial(_ffn_body, seq=seq),
        grid=(t // tm,),
        in_specs=in_specs,
        out_specs=[row(d), st_spec],
        out_shape=[jax.ShapeDtypeStruct((t, d), F32), jax.ShapeDtypeStruct(st_shape, F32)],
        scratch_shapes=scratch,
        compiler_params=_params("arbitrary"),
        name="conv_ffn",
    )(*ins)


def _mla_proj_body(x_ref, g_ref, wdq_ref, gq_ref, wuq_ref, wdkv_ref, gkv_ref, wk_ref, wv_ref,
                   cq_ref, sq_ref, tq_ref, ck_ref, sk_ref, tk_ref,
                   qp_ref, ckv_ref, kpe_ref, kp_ref, v_ref):
    h = _rms(x_ref[...], g_ref[...]).astype(BF16)
    cq = _rms(jnp.dot(h, wdq_ref[...], preferred_element_type=F32), gq_ref[...]).astype(BF16)
    q = jnp.dot(cq, wuq_ref[...], preferred_element_type=F32)
    scale = (C_NOPE + C_ROPE) ** -0.5
    qp_ref[...] = (_rope_apply(q, cq_ref[...], sq_ref[...], tq_ref[...], C_ROPE // 2) * scale).astype(BF16)
    kv = jnp.dot(h, wdkv_ref[...], preferred_element_type=F32)
    ckv = _rms(kv[:, :C_KV_LORA], gkv_ref[...])
    kpe = _rope_apply(kv[:, C_KV_LORA:], ck_ref[...], sk_ref[...], tk_ref[...], C_ROPE // 2)
    ckv_ref[...] = ckv
    kpe_ref[...] = kpe[:, :C_ROPE]
    kin = jnp.concatenate([ckv, kpe], axis=1).astype(BF16)
    kp_ref[...] = jnp.dot(kin, wk_ref[...], preferred_element_type=F32).astype(BF16)
    v_ref[...] = jnp.dot(kin[:, :C_KV_LORA], wv_ref[...], preferred_element_type=F32).astype(BF16)


def _mla_proj(x, g, wdq, gq, wuq_slots, wdkv_pad, gkv, wk_aug, wv, pos):
    t, d = x.shape
    tm = _row_tile(t, 512)
    lane = np.arange(LANES)
    in_q = (lane >= C_NOPE) & (lane < C_NOPE + C_ROPE)
    dq = lane - C_NOPE
    tabs_q = _rope_tables(pos, np.where(in_q, np.where(dq < C_ROPE // 2, 1, 2), 0),
                          np.where(in_q, dq % (C_ROPE // 2), 0), C_ROPE // 2, C_ROPE)
    in_k = lane < C_ROPE
    tabs_k = _rope_tables(pos, np.where(in_k, np.where(lane < C_ROPE // 2, 1, 2), 0),
                          np.where(in_k, lane % (C_ROPE // 2), 0), C_ROPE // 2, C_ROPE)
    row = lambda n: pl.BlockSpec((tm, n), lambda i: (i, 0))
    full = lambda a: pl.BlockSpec(a.shape, lambda i: (0, 0))
    ws = [g, wdq, gq, wuq_slots, wdkv_pad, gkv, wk_aug, wv]
    outs = [(C_HEADS * LANES, BF16), (C_KV_LORA, F32), (C_ROPE, F32), (C_HEADS * LANES, BF16), (C_HEADS * C_V, BF16)]
    return pl.pallas_call(
        _mla_proj_body,
        grid=(t // tm,),
        in_specs=[row(d)] + [full(a) for a in ws] + [row(LANES)] * 6,
        out_specs=[row(n) for n, _ in outs],
        out_shape=[jax.ShapeDtypeStruct((t, n), dt) for n, dt in outs],
        compiler_params=_params("parallel"),
        name="mla_proj",
    )(x, *ws, *tabs_q, *tabs_k)


def _head_mm_body(x_ref, w_ref, o_ref):
    o_ref[0] = jnp.dot(x_ref[0].astype(BF16), w_ref[0], preferred_element_type=F32)


def _head_mm(x, w):
    hh, b, k = x.shape
    n = w.shape[2]
    return pl.pallas_call(
        _head_mm_body,
        grid=(hh,),
        in_specs=[pl.BlockSpec((1, b, k), lambda i: (i, 0, 0)), pl.BlockSpec((1, k, n), lambda i: (i, 0, 0))],
        out_specs=pl.BlockSpec((1, b, n), lambda i: (i, 0, 0)),
        out_shape=jax.ShapeDtypeStruct((hh, b, n), F32),
        compiler_params=_params("parallel"),
        name="head_matmul",
    )(x, w)


def _norm_body(x_ref, g_ref, o_ref):
    o_ref[...] = _rms(x_ref[...], g_ref[...])


def _final_norm(x, g):
    t, d = x.shape
    tm = _row_tile(t, 1024)
    return pl.pallas_call(
        _norm_body,
        grid=(t // tm,),
        in_specs=[pl.BlockSpec((tm, d), lambda i: (i, 0)), pl.BlockSpec(g.shape, lambda i: (0, 0))],
        out_specs=pl.BlockSpec((tm, d), lambda i: (i, 0)),
        out_shape=jax.ShapeDtypeStruct((t, d), F32),
        compiler_params=_params("parallel"),
        name="final_norm",
    )(x, g)


def _pages_per_step(n_pages):
    return min(16, n_pages)


def _page_specs(cp, rows, cols, page_of):
    return [pl.BlockSpec((1, rows, cols), lambda b, c, pt, i=i: (pt[b, page_of(b, c, i)], 0, 0)) for i in range(cp)]


def _top3_ids(cand, lane, out_lane):
    res = jnp.zeros(out_lane.shape, jnp.int32)
    for r in range(MOBA_TOPK):
        mx = jnp.max(cand, axis=-1, keepdims=True)
        first = jnp.min(jnp.where(cand == mx, lane, cand.shape[-1]), axis=-1, keepdims=True)
        res = jnp.where(out_lane == r, first, res)
        cand = jnp.where(lane == first, -jnp.inf, cand)
    return res


def _moba_pick_body(pt_ref, q_ref, *refs, cp):
    k_refs, o_ref, gate_ref = refs[:cp], refs[cp], refs[cp + 1]
    c = pl.program_id(1)
    per = MOBA_BLOCK // PAGE_SIZE

    @pl.when(c == 0)
    def _():
        gate_ref[...] = jnp.full(gate_ref.shape, -jnp.inf, F32)

    qb = q_ref[0].astype(BF16)
    lane = lax.broadcasted_iota(jnp.int32, gate_ref.shape, 1)
    gate = gate_ref[...]
    for i in range(0, cp, per):
        s = sum(jnp.dot(qb, k_refs[i + u][0].astype(BF16), preferred_element_type=F32) for u in range(per))
        val = jnp.sum(s, axis=-1, keepdims=True) * (1.0 / MOBA_BLOCK)
        gate = jnp.where(lane == c * (cp // per) + i // per, val, gate)
    gate_ref[...] = gate

    @pl.when(c == pl.num_programs(1) - 1)
    def _():
        o_ref[0] = _top3_ids(gate, lane, lane)


def _moba_pick(cache_kt, page_table, qbd):
    b, n_pages = page_table.shape
    cp = _pages_per_step(n_pages)
    per = MOBA_BLOCK // PAGE_SIZE
    assert n_pages % cp == 0 and cp % per == 0 and n_pages // per <= LANES
    grid_spec = pltpu.PrefetchScalarGridSpec(
        num_scalar_prefetch=1, grid=(b, n_pages // cp),
        in_specs=[pl.BlockSpec((1, A_HEADS, LANES), lambda bb, c, pt: (bb, 0, 0))]
                 + _page_specs(cp, LANES, PAGE_SIZE, lambda bb, c, i: c * cp + i),
        out_specs=pl.BlockSpec((1, A_HEADS, LANES), lambda bb, c, pt: (bb, 0, 0)),
        scratch_shapes=[pltpu.VMEM((A_HEADS, LANES), F32)])
    return pl.pallas_call(
        functools.partial(_moba_pick_body, cp=cp), grid_spec=grid_spec,
        out_shape=jax.ShapeDtypeStruct((b, A_HEADS, LANES), jnp.int32),
        compiler_params=_params("parallel", "arbitrary"),
        name="moba_decode_pick",
    )(page_table, qbd, *([cache_kt] * cp))


def _moba_attend_body(pg_ref, q_ref, kn_ref, vn_ref, *refs, npg):
    k_refs, v_refs, o_ref = refs[:npg], refs[npg:2 * npg], refs[2 * npg]
    hd = pl.program_id(1)

    @pl.when(hd == 0)
    def _():
        o_ref[...] = jnp.zeros_like(o_ref)

    q = q_ref[0]
    kt = jnp.concatenate([r[0] for r in k_refs], axis=1).astype(BF16)
    vt = jnp.concatenate([r[0] for r in v_refs], axis=1).astype(BF16)
    s = jnp.dot(q.astype(BF16), kt, preferred_element_type=F32)
    s_self = jnp.sum(q * kn_ref[0], axis=-1, keepdims=True)
    m = jnp.maximum(jnp.max(s, axis=-1, keepdims=True), s_self)
    p = jnp.exp(s - m)
    p_self = jnp.exp(s_self - m)
    l = jnp.sum(p, axis=-1, keepdims=True) + p_self
    o = (lax.dot_general(p.astype(BF16), vt, NT_DIMS, preferred_element_type=F32) + p_self * vn_ref[0]) / l
    row = lax.broadcasted_iota(jnp.int32, o.shape, 0)
    o_ref[0] += jnp.where(row == hd, o, 0.0)


def _moba_attend(cache_kt, cache_vt, pages, qbd, k_new, v_new):
    b, _, npg = pages.shape

    def pspec(i):
        return pl.BlockSpec((1, LANES, PAGE_SIZE), lambda bb, h, pg, i=i: (pg[bb, h, i], 0, 0))

    vec = lambda: pl.BlockSpec((1, 1, LANES), lambda bb, h, pg: (bb, 0, 0))
    grid_spec = pltpu.PrefetchScalarGridSpec(
        num_scalar_prefetch=1, grid=(b, A_HEADS),
        in_specs=[pl.BlockSpec((1, A_HEADS, LANES), lambda bb, h, pg: (bb, 0, 0)), vec(), vec()]
                 + [pspec(i) for i in range(npg)] * 2,
        out_specs=pl.BlockSpec((1, A_HEADS, LANES), lambda bb, h, pg: (bb, 0, 0)))
    return pl.pallas_call(
        functools.partial(_moba_attend_body, npg=npg), grid_spec=grid_spec,
        out_shape=jax.ShapeDtypeStruct((b, A_HEADS, LANES), F32),
        compiler_params=_params("parallel", "arbitrary"),
        name="moba_decode_attend",
    )(pages, qbd, k_new[:, None, :], v_new[:, None, :], *([cache_kt] * npg), *([cache_vt] * npg))


def _fox_decode_body(pt_ref, q_ref, kn_ref, vn_ref, lfn_ref, *refs, cp):
    k_refs, v_refs, f_refs = refs[:cp], refs[cp:2 * cp], refs[2 * cp:3 * cp]
    o_ref, m_ref, l_ref, acc_ref, run_ref = refs[3 * cp:]
    c = pl.program_id(1)
    q = q_ref[0]

    @pl.when(c == 0)
    def _():
        m_ref[...] = jnp.sum(q * kn_ref[0], axis=-1, keepdims=True)
        l_ref[...] = jnp.ones_like(l_ref)
        acc_ref[...] = jnp.broadcast_to(vn_ref[0], acc_ref.shape)
        run_ref[...] = lfn_ref[0]

    qb = q.astype(BF16)
    r0 = lax.broadcasted_iota(jnp.int32, (PAGE_SIZE, 2 * PAGE_SIZE), 0)
    r1 = lax.broadcasted_iota(jnp.int32, (PAGE_SIZE, 2 * PAGE_SIZE), 1)
    tri = ((r0 > r1) | (r1 >= PAGE_SIZE)).astype(BF16)
    m, l, acc, run = m_ref[...], l_ref[...], acc_ref[...], run_ref[...]
    for i in range(cp):
        hi, mid, lo = _split3(f_refs[i][0])
        parts = jnp.concatenate([hi.astype(F32), mid.astype(F32), lo.astype(F32)], axis=0).astype(BF16)
        d3 = jnp.dot(parts, tri, preferred_element_type=F32)
        d = d3[0:B_HEADS] + d3[B_HEADS:2 * B_HEADS] + d3[2 * B_HEADS:3 * B_HEADS]
        bias = run + d[:, :PAGE_SIZE]
        run = run + d[:, PAGE_SIZE:PAGE_SIZE + 1]
        s = jnp.dot(qb, k_refs[i][0].astype(BF16), preferred_element_type=F32) + bias
        m_new = jnp.maximum(m, jnp.max(s, axis=-1, keepdims=True))
        p = jnp.exp(s - m_new)
        alpha = jnp.exp(m - m_new)
        l = alpha * l + jnp.sum(p, axis=-1, keepdims=True)
        acc = alpha * acc + lax.dot_general(p.astype(BF16), v_refs[i][0].astype(BF16), NT_DIMS, preferred_element_type=F32)
        m = m_new
    m_ref[...], l_ref[...], acc_ref[...], run_ref[...] = m, l, acc, run

    @pl.when(c == pl.num_programs(1) - 1)
    def _():
        o_ref[0] = acc / l


def _fox_decode(cache_kt, cache_vt, cache_ft, page_table, qbd, k_new, v_new, lf_new):
    b, n_pages = page_table.shape
    cp = _pages_per_step(n_pages)
    assert n_pages % cp == 0
    rev = lambda bb, c, i: n_pages - 1 - (c * cp + i)
    vec = lambda n: pl.BlockSpec((1, 1, n), lambda bb, c, pt: (bb, 0, 0))
    grid_spec = pltpu.PrefetchScalarGridSpec(
        num_scalar_prefetch=1, grid=(b, n_pages // cp),
        in_specs=[pl.BlockSpec((1, B_HEADS, LANES), lambda bb, c, pt: (bb, 0, 0)), vec(LANES), vec(LANES),
                  pl.BlockSpec((1, B_HEADS, 1), lambda bb, c, pt: (bb, 0, 0))]
                 + _page_specs(cp, LANES, PAGE_SIZE, rev) * 2 + _page_specs(cp, B_HEADS, PAGE_SIZE, rev),
        out_specs=pl.BlockSpec((1, B_HEADS, LANES), lambda bb, c, pt: (bb, 0, 0)),
        scratch_shapes=[pltpu.VMEM((B_HEADS, 1), F32), pltpu.VMEM((B_HEADS, 1), F32),
                        pltpu.VMEM((B_HEADS, LANES), F32), pltpu.VMEM((B_HEADS, 1), F32)])
    return pl.pallas_call(
        functools.partial(_fox_decode_body, cp=cp), grid_spec=grid_spec,
        out_shape=jax.ShapeDtypeStruct((b, B_HEADS, LANES), F32),
        compiler_params=_params("parallel", "arbitrary"),
        name="fox_decode",
    )(page_table, qbd, k_new[:, None, :], v_new[:, None, :], lf_new[:, :, None],
      *([cache_kt] * cp), *([cache_vt] * cp), *([cache_ft] * cp))


def _mla_decode_body(pt_ref, ql_ref, qp_ref, cn_ref, pn_ref, *refs, cp):
    c_refs, p_refs = refs[:cp], refs[cp:2 * cp]
    o_ref, m_ref, l_ref, acc_ref = refs[2 * cp:]
    c = pl.program_id(1)
    ql, qp = ql_ref[0], qp_ref[0]

    @pl.when(c == 0)
    def _():
        m_ref[...] = (jnp.sum(ql * cn_ref[0], axis=-1, keepdims=True) + jnp.sum(qp * pn_ref[0], axis=-1, keepdims=True))
        l_ref[...] = jnp.ones_like(l_ref)
        acc_ref[...] = jnp.broadcast_to(cn_ref[0], acc_ref.shape)

    qlb, qpb = ql.astype(BF16), qp.astype(BF16)
    m, l, acc = m_ref[...], l_ref[...], acc_ref[...]
    for i in range(cp):
        ck = c_refs[i][0].astype(BF16)
        s = (lax.dot_general(qlb, ck, NT_DIMS, preferred_element_type=F32)
             + jnp.dot(qpb, p_refs[i][0].astype(BF16), preferred_element_type=F32))
        m_new = jnp.maximum(m, jnp.max(s, axis=-1, keepdims=True))
        p = jnp.exp(s - m_new)
        alpha = jnp.exp(m - m_new)
        l = alpha * l + jnp.sum(p, axis=-1, keepdims=True)
        acc = alpha * acc + jnp.dot(p.astype(BF16), ck, preferred_element_type=F32)
        m = m_new
    m_ref[...], l_ref[...], acc_ref[...] = m, l, acc

    @pl.when(c == pl.num_programs(1) - 1)
    def _():
        o_ref[0] = acc / l


def _mla_decode(cache_c, cache_pt, page_table, q_lat, q_pe, c_new, p_new):
    b, n_pages = page_table.shape
    cp = _pages_per_step(n_pages)
    assert n_pages % cp == 0
    fwd = lambda bb, c, i: c * cp + i
    blk = lambda r, n: pl.BlockSpec((1, r, n), lambda bb, c, pt: (bb, 0, 0))
    grid_spec = pltpu.PrefetchScalarGridSpec(
        num_scalar_prefetch=1, grid=(b, n_pages // cp),
        in_specs=[blk(C_HEADS, C_KV_LORA), blk(C_HEADS, C_ROPE), blk(1, C_KV_LORA), blk(1, C_ROPE)]
                 + _page_specs(cp, PAGE_SIZE, C_KV_LORA, fwd) + _page_specs(cp, C_ROPE, PAGE_SIZE, fwd),
        out_specs=blk(C_HEADS, C_KV_LORA),
        scratch_shapes=[pltpu.VMEM((C_HEADS, 1), F32), pltpu.VMEM((C_HEADS, 1), F32), pltpu.VMEM((C_HEADS, C_KV_LORA), F32)])
    return pl.pallas_call(
        functools.partial(_mla_decode_body, cp=cp), grid_spec=grid_spec,
        out_shape=jax.ShapeDtypeStruct((b, C_HEADS, C_KV_LORA), F32),
        compiler_params=_params("parallel", "arbitrary"),
        name="mla_decode",
    )(page_table, q_lat, q_pe, c_new[:, None, :], p_new[:, None, :], *([cache_c] * cp), *([cache_pt] * cp))


def _feature_major(cache):
    pool, page = cache.shape[:2]
    return cache.transpose(0, 2, 3, 1).reshape(pool, -1, page)


def _block_diag_q(q):
    b = q.shape[0]
    qh = q.astype(F32).reshape(b, 2, 4, HEAD_DIM)
    z = jnp.zeros_like(qh)
    return jnp.concatenate([jnp.concatenate([qh[:, 0], z[:, 0]], axis=-1),
                            jnp.concatenate([z[:, 1], qh[:, 1]], axis=-1)], axis=1)


def _diag_halves(o):
    b = o.shape[0]
    return jnp.concatenate([o[:, :4, :HEAD_DIM].reshape(b, 4 * HEAD_DIM), o[:, 4:, HEAD_DIM:].reshape(b, 4 * HEAD_DIM)], axis=-1)


def kernel(x_prompt, x_sample, cache_moba_k, cache_moba_v, cache_fox_k, cache_fox_v, cache_fox_logf, cache_mla_ckv,
           cache_mla_kpe, state_ffn_conv, page_table, norm_mix, norm_ffn, norm_final, w_in_even, b_forget, w_out_even,
           w_dq, g_q, w_uq, w_dkv, g_kv, w_uk, w_uv, w_out_odd, w_gate, w_up, conv_w, conv_b, w_down):
    bp, sp, d = x_prompt.shape
    bs, ts, _ = x_sample.shape
    assert bp == 1 and ts == 1
    n_pages = page_table.shape[1]
    past = n_pages * PAGE_SIZE
    assert past % MOBA_BLOCK == 0 and past // MOBA_BLOCK >= MOBA_TOPK and sp % MOBA_BLOCK == 0
    depth = norm_mix.shape[0]
    pos_p = jnp.arange(sp, dtype=jnp.int32)
    pos_s = jnp.full((bs,), past, jnp.int32)
    xp = x_prompt[0]
    xs = x_sample[:, 0]
    outs_p = {k: [] for k in ("mk", "mv", "fk", "fv", "fl", "ck", "kp", "cv")}
    outs_s = {k: [] for k in ("mk", "mv", "fk", "fv", "fl", "ck", "kp", "cv")}
    row2 = lambda v: v.reshape(1, -1)

    for layer in range(depth):
        if layer % 2 == 0:
            e = layer // 2
            w = w_in_even[e]
            w_cat = jnp.concatenate([w, jnp.zeros((d, LANES - B_HEADS), F32)], axis=1).astype(BF16)
            bf_pad = jnp.concatenate([b_forget[e], jnp.zeros((LANES - B_HEADS,), F32)]).reshape(1, LANES)
            g = row2(norm_mix[layer])
            w_out = w_out_even[e].astype(BF16)
            qa, ka, va, qb, kb, vb, lf, chi, cmid, clo = _even_proj(xp, g, w_cat, bf_pad, pos_p)
            nb = sp // MOBA_BLOCK
            bias = _moba_gate(qa, _block_mean(ka))
            qpa = jnp.concatenate([qa.reshape(sp, A_HEADS, HEAD_DIM), bias.reshape(sp, A_HEADS, HEAD_DIM)], axis=-1)
            onehot = (pos_p[:, None] // MOBA_BLOCK == jnp.arange(HEAD_DIM)[None, :]).astype(BF16)
            kpa = jnp.concatenate([ka.astype(BF16).reshape(sp, 2, HEAD_DIM).transpose(1, 0, 2),
                                   jnp.broadcast_to(onehot[None], (2, sp, HEAD_DIM))], axis=-1)
            oa = _flash(qpa.reshape(sp, A_HEADS * LANES), kpa, va.astype(BF16).T, g=4, tq=512, tk=512)
            cparts = jnp.stack([chi, cmid, clo], axis=-1)
            sel = jnp.tile(jnp.repeat(jnp.eye(4, dtype=BF16), 3, axis=1), (2, 1))
            aug_q = jnp.concatenate([jnp.broadcast_to(sel[None], (sp, B_HEADS, 12)), cparts,
                                     jnp.zeros((sp, B_HEADS, HEAD_DIM - 15), BF16)], axis=-1)
            qpb = jnp.concatenate([qb.reshape(sp, B_HEADS, HEAD_DIM), aug_q], axis=-1)
            aug_k = jnp.concatenate([(-cparts).reshape(sp, 2, 12), jnp.ones((sp, 2, 3), BF16),
                                     jnp.zeros((sp, 2, HEAD_DIM - 15), BF16)], axis=-1)
            kpb = jnp.concatenate([kb.astype(BF16).reshape(sp, 2, HEAD_DIM), aug_k], axis=-1).transpose(1, 0, 2)
            ob = _flash(qpb.reshape(sp, B_HEADS * LANES), kpb, vb.astype(BF16).T, g=4, tq=512, tk=512)
            xp = _out_proj(xp, [oa, ob], [w_out[:A_HEADS * HEAD_DIM], w_out[A_HEADS * HEAD_DIM:]])
            for key, val in zip(("mk", "mv", "fk", "fv"), (ka, va, kb, vb)):
                outs_p[key].append(val.reshape(1, sp, 2, HEAD_DIM))
            outs_p["fl"].append(lf.reshape(1, sp, B_HEADS))
            qa, ka, va, qb, kb, vb, lf, _, _, _ = _even_proj(xs, g, w_cat, bf_pad, pos_s)
            ck, cv = _feature_major(cache_moba_k[e]), _feature_major(cache_moba_v[e])
            qbd = _block_diag_q(qa)
            picks = _moba_pick(ck, page_table, qbd)[:, :, :MOBA_TOPK]
            per = MOBA_BLOCK // PAGE_SIZE
            logical = (picks[..., None] * per + jnp.arange(per, dtype=jnp.int32)).reshape(bs, A_HEADS, MOBA_TOPK * per)
            pages = jnp.take_along_axis(page_table[:, None, :], logical, axis=2)
            oa = _diag_halves(_moba_attend(ck, cv, pages, qbd, ka, va)).astype(BF16)
            ob = _fox_decode(_feature_major(cache_fox_k[e]), _feature_major(cache_fox_v[e]),
                             cache_fox_logf[e].transpose(0, 2, 1), page_table, _block_diag_q(qb), kb, vb, lf)
            ob = _diag_halves(ob).astype(BF16)
            xs = _out_proj(xs, [oa, ob], [w_out[:A_HEADS * HEAD_DIM], w_out[A_HEADS * HEAD_DIM:]])
            for key, val in zip(("mk", "mv", "fk", "fv"), (ka, va, kb, vb)):
                outs_s[key].append(val.reshape(bs, 1, 2, HEAD_DIM))
            outs_s["fl"].append(lf.reshape(bs, 1, B_HEADS))
        else:
            o = layer // 2
            dq = C_NOPE + C_ROPE
            wq = w_uq[o].reshape(C_Q_LORA, C_HEADS, dq)
            wuq_slots = jnp.concatenate([wq, jnp.zeros((C_Q_LORA, C_HEADS, LANES - dq), F32)], axis=-1)
            wuq_slots = wuq_slots.reshape(C_Q_LORA, C_HEADS * LANES).astype(BF16)
            wdkv_pad = jnp.concatenate([w_dkv[o], jnp.zeros((d, LANES - C_ROPE), F32)], axis=1).astype(BF16)
            wk_top = jnp.concatenate([w_uk[o], jnp.zeros((C_KV_LORA, C_HEADS, LANES - C_NOPE), F32)], axis=-1)
            place = jnp.concatenate([jnp.zeros((C_ROPE, C_NOPE), F32), jnp.eye(C_ROPE, dtype=F32),
                                     jnp.zeros((C_ROPE, LANES - dq), F32)], axis=1)
            wk_mid = jnp.broadcast_to(place[:, None, :], (C_ROPE, C_HEADS, LANES))
            wk_aug = jnp.concatenate([wk_top, wk_mid, jnp.zeros((LANES - C_ROPE, C_HEADS, LANES), F32)], axis=0)
            wk_aug = wk_aug.reshape(C_KV_LORA + LANES, C_HEADS * LANES).astype(BF16)
            wv = w_uv[o].reshape(C_KV_LORA, C_HEADS * C_V).astype(BF16)
            ws = (row2(norm_mix[layer]), w_dq[o].astype(BF16), row2(g_q[o]), wuq_slots, wdkv_pad, row2(g_kv[o]), wk_aug, wv)
            w_out = w_out_odd[o].astype(BF16)
            qp, ckv, kpe, kp, v = _mla_proj(xp, *ws, pos_p)
            oc = _flash(qp, kp.reshape(sp, C_HEADS, LANES).transpose(1, 0, 2), v.T, g=1, tq=1024, tk=512)
            xp = _out_proj(xp, [oc], [w_out])
            outs_p["ck"].append(ckv.reshape(1, sp, C_KV_LORA))
            outs_p["kp"].append(kpe.reshape(1, sp, C_ROPE))
            qp, ckv, kpe, _, _ = _mla_proj(xs, *ws, pos_s)
            qh = qp.reshape(bs, C_HEADS, LANES).transpose(1, 0, 2)
            q_lat = _head_mm(qh[:, :, :C_NOPE], w_uk[o].transpose(1, 2, 0).astype(BF16))
            o_lat = _mla_decode(cache_mla_ckv[o], cache_mla_kpe[o].transpose(0, 2, 1), page_table, q_lat.transpose(1, 0, 2),
                                qh[:, :, C_NOPE:dq].transpose(1, 0, 2).astype(F32), ckv, kpe)
            oc = _head_mm(o_lat.transpose(1, 0, 2), w_uv[o].transpose(1, 0, 2).astype(BF16))
            xs = _out_proj(xs, [oc.transpose(1, 0, 2).reshape(bs, C_HEADS * C_V).astype(BF16)], [w_out])
            outs_s["ck"].append(ckv.reshape(bs, 1, C_KV_LORA))
            outs_s["kp"].append(kpe.reshape(bs, 1, C_ROPE))
        ffn_w = (row2(norm_ffn[layer]), w_gate[layer].astype(BF16), w_up[layer].astype(BF16), conv_w[layer],
                 row2(conv_b[layer]), w_down[layer].astype(BF16))
        xp, st = _ffn(xp, *ffn_w)
        outs_p["cv"].append(st[8 - (CONV_W - 1):].reshape(1, CONV_W - 1, -1))
        xs, gate_rows = _ffn(xs, *ffn_w, state=state_ffn_conv[layer])
        outs_s["cv"].append(jnp.stack([state_ffn_conv[layer][:, 1], gate_rows], axis=1))

    gf = row2(norm_final)
    order = ("mk", "mv", "fk", "fv", "fl", "ck", "kp", "cv")
    return ((_final_norm(xp, gf).reshape(bp, sp, d), _final_norm(xs, gf).reshape(bs, ts, d))
            + tuple(jnp.stack(outs_p[k]) for k in order) + tuple(jnp.stack(outs_s[k]) for k in order))
```

```python
import functools
import math

import numpy as np
import jax
import jax.numpy as jnp
from jax import lax
from jax.experimental import pallas as pl
from jax.experimental.pallas import tpu as pltpu

F32 = jnp.float32
BF16 = jnp.bfloat16

HEAD_DIM = 64
A_HEADS, A_KV_HEADS = 8, 2
B_HEADS, B_KV_HEADS = 8, 2
C_HEADS, C_Q_LORA, C_KV_LORA, C_NOPE, C_ROPE, C_V = 16, 512, 256, 64, 32, 64
MOBA_BLOCK, MOBA_TOPK = 256, 3
PAGE_SIZE = 128
CONV_W = 3
ROPE_THETA = 10000.0
EPS = 1e-6
LANES = 128
NEG = -1e30
LOG2E = math.log2(math.e)
VMEM_LIMIT = 56 * 1024 * 1024
HIGHEST = lax.Precision.HIGHEST
NT_DIMS = (((1,), (1,)), ((), ()))
TN_DIMS = (((0,), (0,)), ((), ()))


def _params(*sem):
    return pltpu.CompilerParams(dimension_semantics=sem, vmem_limit_bytes=VMEM_LIMIT)


def _rms(x, g):
    return x * lax.rsqrt(jnp.mean(x * x, axis=-1, keepdims=True) + EPS) * g


def _rope_tables(pos, kind, freq, half, d):
    inv = ROPE_THETA ** (-jnp.arange(half, dtype=F32) * (2.0 / d))
    ang = pos.astype(F32)[:, None] * inv[None, :]
    cos, sin = jnp.cos(ang)[:, freq], jnp.sin(ang)[:, freq]
    kind = jnp.asarray(kind)[None, :]
    return (jnp.where(kind == 0, 1.0, cos).astype(F32), jnp.where(kind == 1, -sin, 0.0).astype(F32),
            jnp.where(kind == 2, sin, 0.0).astype(F32))


def _rope_apply(z, cos, sa, sb, half):
    outs = []
    for c in range(z.shape[1] // LANES):
        x = z[:, c * LANES:(c + 1) * LANES]
        outs.append(x * cos + pltpu.roll(x, LANES - half, 1) * sa + pltpu.roll(x, half, 1) * sb)
    return outs[0] if len(outs) == 1 else jnp.concatenate(outs, axis=1)


def _split3(x):
    hi = x.astype(BF16)
    r = x - hi.astype(F32)
    mid = r.astype(BF16)
    lo = (r - mid.astype(F32)).astype(BF16)
    return hi, mid, lo


def _row_tile(t, want):
    return want if t % want == 0 else t


def _even_proj_body(x_ref, g_ref, w_ref, bf_ref, cos_ref, sa_ref, sb_ref,
                    qa_ref, ka_ref, va_ref, qb_ref, kb_ref, vb_ref, lf_ref, chi_ref, cmid_ref, clo_ref, carry_ref):
    @pl.when(pl.program_id(0) == 0)
    def _():
        carry_ref[...] = jnp.zeros_like(carry_ref)

    tm = x_ref.shape[0]
    h = _rms(x_ref[...], g_ref[...]).astype(BF16)
    z = jnp.dot(h, w_ref[...], preferred_element_type=F32)
    cos, sa, sb = cos_ref[...], sa_ref[...], sb_ref[...]
    half = HEAD_DIM // 2
    scale = HEAD_DIM ** -0.5 * LOG2E
    o = 0
    qa_ref[...] = (_rope_apply(z[:, o:o + 512], cos, sa, sb, half) * scale).astype(BF16); o += 512
    ka_ref[...] = _rope_apply(z[:, o:o + 128], cos, sa, sb, half); o += 128
    va_ref[...] = z[:, o:o + 128]; o += 128
    qb_ref[...] = (z[:, o:o + 512] * scale).astype(BF16); o += 512
    kb_ref[...] = z[:, o:o + 128]; o += 128
    vb_ref[...] = z[:, o:o + 128]; o += 128
    fl = z[:, o:o + 128] + bf_ref[...]
    lf = jnp.minimum(fl, 0.0) - jnp.log1p(jnp.exp(-jnp.abs(fl)))
    lf_ref[...] = lf[:, :B_HEADS]
    tri = (lax.broadcasted_iota(jnp.int32, (tm, tm), 0) >= lax.broadcasted_iota(jnp.int32, (tm, tm), 1)).astype(F32)
    cum = jnp.dot(tri, lf, preferred_element_type=F32, precision=HIGHEST) + carry_ref[...]
    carry_ref[...] = cum[tm - 1:tm, :]
    hi, mid, lo = _split3(cum[:, :B_HEADS] * LOG2E)
    chi_ref[...] = hi
    cmid_ref[...] = mid
    clo_ref[...] = lo


def _even_proj(x, g, w_cat, bf_pad, pos):
    t, d = x.shape
    tm = _row_tile(t, 512)
    lane = np.arange(LANES) % HEAD_DIM
    cos, sa, sb = _rope_tables(pos, np.where(lane < 32, 1, 2), lane % 32, HEAD_DIM // 2, HEAD_DIM)
    row = lambda n: pl.BlockSpec((tm, n), lambda i: (i, 0))
    full = lambda a: pl.BlockSpec(a.shape, lambda i: (0, 0))
    outs = [(512, BF16), (128, F32), (128, F32), (512, BF16), (128, F32), (128, F32), (B_HEADS, F32),
            (B_HEADS, BF16), (B_HEADS, BF16), (B_HEADS, BF16)]
    return pl.pallas_call(
        _even_proj_body,
        grid=(t // tm,),
        in_specs=[row(d), full(g), full(w_cat), full(bf_pad), row(LANES), row(LANES), row(LANES)],
        out_specs=[row(n) for n, _ in outs],
        out_shape=[jax.ShapeDtypeStruct((t, n), dt) for n, dt in outs],
        scratch_shapes=[pltpu.VMEM((1, LANES), F32)],
        compiler_params=_params("arbitrary"),
        name="even_proj",
    )(x, g, w_cat, bf_pad, cos, sa, sb)


def _block_mean_body(k_ref, o_ref):
    nb = o_ref.shape[0]
    o_ref[...] = jnp.mean(k_ref[...].reshape(nb, MOBA_BLOCK, LANES), axis=1)


def _block_mean(k):
    t = k.shape[0]
    nb = t // MOBA_BLOCK
    return pl.pallas_call(
        _block_mean_body,
        out_shape=jax.ShapeDtypeStruct((nb, LANES), F32),
        compiler_params=_params(),
        name="moba_block_mean",
    )(k)


def _top3_bias(gate, n_valid, lane):
    cand = jnp.where(lane < n_valid, gate, -jnp.inf)
    keep = lane >= n_valid
    for _ in range(MOBA_TOPK):
        mx = jnp.max(cand, axis=-1, keepdims=True)
        first = jnp.min(jnp.where(cand == mx, lane, gate.shape[-1]), axis=-1, keepdims=True)
        hit = (lane == first) & (mx > -jnp.inf)
        keep = keep | hit
        cand = jnp.where(hit, -jnp.inf, cand)
    return jnp.where(keep, 0.0, NEG)


def _moba_gate_body(q_ref, km_ref, o_ref):
    tm = q_ref.shape[0]
    nb = km_ref.shape[0]
    g = A_HEADS // A_KV_HEADS
    pos = pl.program_id(0) * tm + lax.broadcasted_iota(jnp.int32, (tm, 1), 0)
    own = pos // MOBA_BLOCK
    lane = lax.broadcasted_iota(jnp.int32, (tm, nb), 1)
    for h in range(A_HEADS):
        q = q_ref[:, h * HEAD_DIM:(h + 1) * HEAD_DIM].astype(F32)
        km = km_ref[:, (h // g) * HEAD_DIM:(h // g + 1) * HEAD_DIM]
        gate = lax.dot_general(q, km, NT_DIMS, preferred_element_type=F32, precision=HIGHEST)
        bias = _top3_bias(gate, own, lane)
        if nb < HEAD_DIM:
            bias = jnp.concatenate([bias, jnp.zeros((tm, HEAD_DIM - nb), F32)], axis=1)
        o_ref[:, h * HEAD_DIM:(h + 1) * HEAD_DIM] = bias.astype(BF16)


def _moba_gate(q, kmean):
    t = q.shape[0]
    tm = _row_tile(t, 512)
    assert kmean.shape[0] <= HEAD_DIM, "one bias lane per key block"
    return pl.pallas_call(
        _moba_gate_body,
        grid=(t // tm,),
        in_specs=[pl.BlockSpec((tm, q.shape[1]), lambda i: (i, 0)), pl.BlockSpec(kmean.shape, lambda i: (0, 0))],
        out_specs=pl.BlockSpec((tm, A_HEADS * HEAD_DIM), lambda i: (i, 0)),
        out_shape=jax.ShapeDtypeStruct((t, A_HEADS * HEAD_DIM), BF16),
        compiler_params=_params("parallel"),
        name="moba_gate",
    )(q, kmean)


FLASH_ROWS, FLASH_COLS = 16, 512


def _flash_body(q_ref, k_ref, vt_ref, o_ref, s_ref, p_ref, *, g, tq, tk):
    i = pl.program_id(1)
    r = g * tq
    n_full = (i * tq) // tk
    n_mask = tq // tk
    qs = [jnp.concatenate([q_ref[:, (s * g + a) * LANES:(s * g + a + 1) * LANES] for a in range(g)], axis=0)
          if g > 1 else q_ref[:, s * LANES:(s + 1) * LANES] for s in range(2)]

    def tile(j, carry, masked):
        start = pl.multiple_of(j * tk, tk)
        out = []
        for s in range(2):
            m, l, acc = carry[s]
            k = k_ref[s, pl.ds(start, tk), :]
            st = lax.dot_general(k, qs[s], NT_DIMS, preferred_element_type=F32)
            if masked:
                kpos = start + lax.broadcasted_iota(jnp.int32, (tk, r), 0)
                qpos = i * tq + lax.broadcasted_iota(jnp.int32, (tk, r), 1) % tq
                st = jnp.where(kpos <= qpos, st, NEG)
            s_ref[s] = st
            mx = []
            for c0 in range(0, r, FLASH_COLS):
                part = s_ref[s, 0:FLASH_ROWS, c0:c0 + FLASH_COLS]
                for r0 in range(FLASH_ROWS, tk, FLASH_ROWS):
                    part = jnp.maximum(part, s_ref[s, r0:r0 + FLASH_ROWS, c0:c0 + FLASH_COLS])
                mx.append(jnp.max(part, axis=0, keepdims=True))
            m_new = jnp.maximum(m, jnp.concatenate(mx, axis=1))
            psum = []
            for c0 in range(0, r, FLASH_COLS):
                mb = jnp.broadcast_to(m_new[:, c0:c0 + FLASH_COLS], (FLASH_ROWS, FLASH_COLS))
                part = jnp.zeros((FLASH_ROWS, FLASH_COLS), F32)
                for r0 in range(0, tk, FLASH_ROWS):
                    p = jnp.exp2(s_ref[s, r0:r0 + FLASH_ROWS, c0:c0 + FLASH_COLS] - mb)
                    part = part + p
                    p_ref[s, r0:r0 + FLASH_ROWS, c0:c0 + FLASH_COLS] = p.astype(BF16)
                psum.append(jnp.sum(part, axis=0, keepdims=True))
            alpha = jnp.exp2(m - m_new)
            l = alpha * l + jnp.concatenate(psum, axis=1)
            vt = vt_ref[s * HEAD_DIM:(s + 1) * HEAD_DIM, pl.ds(start, tk)]
            acc = alpha * acc + jnp.dot(vt, p_ref[s], preferred_element_type=F32)
            out.append((m_new, l, acc))
        return tuple(out)

    init = tuple((jnp.full((1, r), NEG, F32), jnp.zeros((1, r), F32), jnp.zeros((HEAD_DIM, r), F32)) for _ in range(2))
    carry = lax.fori_loop(0, n_full, lambda j, c: tile(j, c, False), init)
    for jm in range(n_mask):
        carry = tile(n_full + jm, carry, True)
    for s in range(2):
        _, l, acc = carry[s]
        o = acc / l
        for a in range(g):
            col = (s * g + a) * HEAD_DIM
            o_ref[:, col:col + HEAD_DIM] = o[:, a * tq:(a + 1) * tq].T.astype(BF16)


def _flash(qp, kp, vt, *, g, tq, tk):
    t = qp.shape[0]
    nc = kp.shape[0] // 2
    tq, tk = min(tq, t), min(tk, t)
    assert t % tq == 0 and tq % tk == 0 and tk % FLASH_ROWS == 0 and (g * tq) % FLASH_COLS == 0
    return pl.pallas_call(
        functools.partial(_flash_body, g=g, tq=tq, tk=tk),
        grid=(nc, t // tq),
        in_specs=[pl.BlockSpec((tq, 2 * g * LANES), lambda c, i: (i, c)),
                  pl.BlockSpec((2, t, LANES), lambda c, i: (c, 0, 0)),
                  pl.BlockSpec((LANES, t), lambda c, i: (c, 0))],
        out_specs=pl.BlockSpec((tq, 2 * g * HEAD_DIM), lambda c, i: (i, c)),
        out_shape=jax.ShapeDtypeStruct((t, nc * 2 * g * HEAD_DIM), BF16),
        scratch_shapes=[pltpu.VMEM((2, tk, g * tq), F32), pltpu.VMEM((2, tk, g * tq), BF16)],
        compiler_params=_params("parallel", "arbitrary"),
        name="flash_attention",
    )(qp, kp, vt)


def _out_proj_body(*refs, n):
    x_ref, o_ref = refs[0], refs[-1]
    acc = x_ref[...]
    for a_ref, w_ref in zip(refs[1:1 + n], refs[1 + n:1 + 2 * n]):
        acc = acc + jnp.dot(a_ref[...], w_ref[...], preferred_element_type=F32)
    o_ref[...] = acc


def _out_proj(x, acts, ws):
    t, d = x.shape
    tm = _row_tile(t, 512)
    return pl.pallas_call(
        functools.partial(_out_proj_body, n=len(acts)),
        grid=(t // tm,),
        in_specs=[pl.BlockSpec((tm, d), lambda i: (i, 0))]
                 + [pl.BlockSpec((tm, a.shape[1]), lambda i: (i, 0)) for a in acts]
                 + [pl.BlockSpec(w.shape, lambda i: (0, 0)) for w in ws],
        out_specs=pl.BlockSpec((tm, d), lambda i: (i, 0)),
        out_shape=jax.ShapeDtypeStruct((t, d), F32),
        compiler_params=_params("parallel"),
        name="out_proj",
    )(x, *acts, *ws)


FF_CHUNK = 256


def _ffn_body(*refs, seq):
    if seq:
        x_ref, gn_ref, wg_ref, wu_ref, cw_ref, cb_ref, wd_ref, y_ref, st_ref, carry_ref = refs
    else:
        x_ref, gn_ref, wg_ref, wu_ref, cw_ref, cb_ref, wd_ref, s0_ref, s1_ref, y_ref, st_ref = refs
    tm = x_ref.shape[0]
    ff = wg_ref.shape[1]
    if seq:
        @pl.when(pl.program_id(0) == 0)
        def _():
            carry_ref[...] = jnp.zeros_like(carry_ref)
        row = lax.broadcasted_iota(jnp.int32, (tm, FF_CHUNK), 0)

    x = x_ref[...]
    h = _rms(x, gn_ref[...]).astype(BF16)
    acc = x
    for c in range(ff // FF_CHUNK):
        sl = slice(c * FF_CHUNK, (c + 1) * FF_CHUNK)
        gch = jnp.dot(h, wg_ref[:, sl], preferred_element_type=F32)
        uch = jnp.dot(h, wu_ref[:, sl], preferred_element_type=F32)
        if seq:
            prev = carry_ref[:, sl]
            g1 = jnp.where(row == 0, prev[7:8, :], pltpu.roll(gch, 1, 0))
            g2 = jnp.where(row == 0, prev[6:7, :], jnp.where(row == 1, prev[7:8, :], pltpu.roll(gch, 2, 0)))
            carry_ref[:, sl] = gch[tm - 8:tm, :]
            st_ref[:, sl] = gch[tm - 8:tm, :]
        else:
            g1, g2 = s1_ref[:, sl], s0_ref[:, sl]
            st_ref[:, sl] = gch
        gc = cb_ref[:, sl] + cw_ref[0:1, sl] * g2 + cw_ref[1:2, sl] * g1 + cw_ref[2:3, sl] * gch
        act = gc / (1.0 + jnp.exp(-gc)) * uch
        acc = acc + jnp.dot(act.astype(BF16), wd_ref[sl, :], preferred_element_type=F32)
    y_ref[...] = acc


def _ffn(x, gn, wg, wu, cw, cb, wd, state=None):
    t, d = x.shape
    ff = wg.shape[1]
    assert ff % FF_CHUNK == 0
    seq = state is None
    tm = _row_tile(t, 512) if seq else t
    assert tm >= 8
    row = lambda n: pl.BlockSpec((tm, n), lambda i: (i, 0))
    full = lambda a: pl.BlockSpec(a.shape, lambda i: (0, 0))
    ins = [x, gn, wg, wu, cw, cb, wd]
    in_specs = [row(d)] + [full(a) for a in ins[1:]]
    if seq:
        st_shape, st_spec = (8, ff), pl.BlockSpec((8, ff), lambda i: (0, 0))
        scratch = [pltpu.VMEM((8, ff), F32)]
    else:
        ins += [state[:, 0], state[:, 1]]
        in_specs += [row(ff), row(ff)]
        st_shape, st_spec = (t, ff), row(ff)
        scratch = []
    return pl.pallas_call(
        functools.partial(_ffn_body, seq=seq),
        grid=(t // tm,),
        in_specs=in_specs,
        out_specs=[row(d), st_spec],
        out_shape=[jax.ShapeDtypeStruct((t, d), F32), jax.ShapeDtypeStruct(st_shape, F32)],
        scratch_shapes=scratch,
        compiler_params=_params("arbitrary"),
        name="conv_ffn",
    )(*ins)


def _mla_proj_body(x_ref, g_ref, wdq_ref, gq_ref, wuq_ref, wdkv_ref, gkv_ref, wk_ref, wv_ref,
                   cq_ref, sq_ref, tq_ref, ck_ref, sk_ref, tk_ref,
                   qp_ref, ckv_ref, kpe_ref, kp_ref, v_ref):
    h = _rms(x_ref[...], g_ref[...]).astype(BF16)
    cq = _rms(jnp.dot(h, wdq_ref[...], preferred_element_type=F32), gq_ref[...]).astype(BF16)
    q = jnp.dot(cq, wuq_ref[...], preferred_element_type=F32)
    scale = (C_NOPE + C_ROPE) ** -0.5 * LOG2E
    qp_ref[...] = (_rope_apply(q, cq_ref[...], sq_ref[...], tq_ref[...], C_ROPE // 2) * scale).astype(BF16)
    kv = jnp.dot(h, wdkv_ref[...], preferred_element_type=F32)
    ckv = _rms(kv[:, :C_KV_LORA], gkv_ref[...])
    kpe = _rope_apply(kv[:, C_KV_LORA:], ck_ref[...], sk_ref[...], tk_ref[...], C_ROPE // 2)
    ckv_ref[...] = ckv
    kpe_ref[...] = kpe[:, :C_ROPE]
    kin = jnp.concatenate([ckv, kpe], axis=1).astype(BF16)
    kp_ref[...] = jnp.dot(kin, wk_ref[...], preferred_element_type=F32).astype(BF16)
    v_ref[...] = jnp.dot(kin[:, :C_KV_LORA], wv_ref[...], preferred_element_type=F32).astype(BF16)


def _mla_proj(x, g, wdq, gq, wuq_slots, wdkv_pad, gkv, wk_aug, wv, pos):
    t, d = x.shape
    tm = _row_tile(t, 512)
    lane = np.arange(LANES)
    in_q = (lane >= C_NOPE) & (lane < C_NOPE + C_ROPE)
    dq = lane - C_NOPE
    tabs_q = _rope_tables(pos, np.where(in_q, np.where(dq < C_ROPE // 2, 1, 2), 0),
                          np.where(in_q, dq % (C_ROPE // 2), 0), C_ROPE // 2, C_ROPE)
    in_k = lane < C_ROPE
    tabs_k = _rope_tables(pos, np.where(in_k, np.where(lane < C_ROPE // 2, 1, 2), 0),
                          np.where(in_k, lane % (C_ROPE // 2), 0), C_ROPE // 2, C_ROPE)
    row = lambda n: pl.BlockSpec((tm, n), lambda i: (i, 0))
    full = lambda a: pl.BlockSpec(a.shape, lambda i: (0, 0))
    ws = [g, wdq, gq, wuq_slots, wdkv_pad, gkv, wk_aug, wv]
    outs = [(C_HEADS * LANES, BF16), (C_KV_LORA, F32), (C_ROPE, F32), (C_HEADS * LANES, BF16), (C_HEADS * C_V, BF16)]
    return pl.pallas_call(
        _mla_proj_body,
        grid=(t // tm,),
        in_specs=[row(d)] + [full(a) for a in ws] + [row(LANES)] * 6,
        out_specs=[row(n) for n, _ in outs],
        out_shape=[jax.ShapeDtypeStruct((t, n), dt) for n, dt in outs],
        compiler_params=_params("parallel"),
        name="mla_proj",
    )(x, *ws, *tabs_q, *tabs_k)


def _head_mm_body(x_ref, w_ref, o_ref):
    o_ref[0] = jnp.dot(x_ref[0].astype(BF16), w_ref[0], preferred_element_type=F32)


def _head_mm(x, w):
    hh, b, k = x.shape
    n = w.shape[2]
    return pl.pallas_call(
        _head_mm_body,
        grid=(hh,),
        in_specs=[pl.BlockSpec((1, b, k), lambda i: (i, 0, 0)), pl.BlockSpec((1, k, n), lambda i: (i, 0, 0))],
        out_specs=pl.BlockSpec((1, b, n), lambda i: (i, 0, 0)),
        out_shape=jax.ShapeDtypeStruct((hh, b, n), F32),
        compiler_params=_params("parallel"),
        name="head_matmul",
    )(x, w)


def _norm_body(x_ref, g_ref, o_ref):
    o_ref[...] = _rms(x_ref[...], g_ref[...])


def _final_norm(x, g):
    t, d = x.shape
    tm = _row_tile(t, 1024)
    return pl.pallas_call(
        _norm_body,
        grid=(t // tm,),
        in_specs=[pl.BlockSpec((tm, d), lambda i: (i, 0)), pl.BlockSpec(g.shape, lambda i: (0, 0))],
        out_specs=pl.BlockSpec((tm, d), lambda i: (i, 0)),
        out_shape=jax.ShapeDtypeStruct((t, d), F32),
        compiler_params=_params("parallel"),
        name="final_norm",
    )(x, g)


def _pages_per_step(n_pages):
    return min(16, n_pages)


def _page_specs(cp, rows, cols, page_of):
    return [pl.BlockSpec((1, rows, cols), lambda b, c, pt, i=i: (pt[b, page_of(b, c, i)], 0, 0)) for i in range(cp)]


def _top3_ids(cand, lane, out_lane):
    res = jnp.zeros(out_lane.shape, jnp.int32)
    for r in range(MOBA_TOPK):
        mx = jnp.max(cand, axis=-1, keepdims=True)
        first = jnp.min(jnp.where(cand == mx, lane, cand.shape[-1]), axis=-1, keepdims=True)
        res = jnp.where(out_lane == r, first, res)
        cand = jnp.where(lane == first, -jnp.inf, cand)
    return res


def _moba_pick_body(pt_ref, q_ref, *refs, cp):
    k_refs, o_ref, gate_ref = refs[:cp], refs[cp], refs[cp + 1]
    c = pl.program_id(1)
    per = MOBA_BLOCK // PAGE_SIZE

    @pl.when(c == 0)
    def _():
        gate_ref[...] = jnp.full(gate_ref.shape, -jnp.inf, F32)

    qb = q_ref[0].astype(BF16)
    lane = lax.broadcasted_iota(jnp.int32, gate_ref.shape, 1)
    gate = gate_ref[...]
    kt = jnp.concatenate([r[0].astype(BF16) for r in k_refs], axis=1)
    s = jnp.dot(qb, kt, preferred_element_type=F32)
    for n in range(cp // per):
        val = jnp.sum(s[:, n * MOBA_BLOCK:(n + 1) * MOBA_BLOCK], axis=-1, keepdims=True) * (1.0 / MOBA_BLOCK)
        gate = jnp.where(lane == c * (cp // per) + n, val, gate)
    gate_ref[...] = gate

    @pl.when(c == pl.num_programs(1) - 1)
    def _():
        o_ref[0] = _top3_ids(gate, lane, lane)


def _moba_pick(cache_kt, page_table, qbd):
    b, n_pages = page_table.shape
    cp = _pages_per_step(n_pages)
    per = MOBA_BLOCK // PAGE_SIZE
    assert n_pages % cp == 0 and cp % per == 0 and n_pages // per <= LANES
    grid_spec = pltpu.PrefetchScalarGridSpec(
        num_scalar_prefetch=1, grid=(b, n_pages // cp),
        in_specs=[pl.BlockSpec((1, A_HEADS, LANES), lambda bb, c, pt: (bb, 0, 0))]
                 + _page_specs(cp, LANES, PAGE_SIZE, lambda bb, c, i: c * cp + i),
        out_specs=pl.BlockSpec((1, A_HEADS, LANES), lambda bb, c, pt: (bb, 0, 0)),
        scratch_shapes=[pltpu.VMEM((A_HEADS, LANES), F32)])
    return pl.pallas_call(
        functools.partial(_moba_pick_body, cp=cp), grid_spec=grid_spec,
        out_shape=jax.ShapeDtypeStruct((b, A_HEADS, LANES), jnp.int32),
        compiler_params=_params("parallel", "arbitrary"),
        name="moba_decode_pick",
    )(page_table, qbd, *([cache_kt] * cp))


def _moba_attend_body(pg_ref, q_ref, kn_ref, vn_ref, *refs, npg, gsz):
    n = gsz * npg
    k_refs, v_refs, o_ref = refs[:n], refs[n:2 * n], refs[2 * n]
    q = q_ref[0, 0]
    qb = q.astype(BF16)
    s_self = jnp.sum(q * kn_ref[0, 0], axis=-1, keepdims=True)
    row = lax.broadcasted_iota(jnp.int32, (q.shape[0], HEAD_DIM), 0)
    out = jnp.zeros((q.shape[0], HEAD_DIM), F32)
    for a in range(gsz):
        kt = jnp.concatenate([r[0].astype(BF16) for r in k_refs[a * npg:(a + 1) * npg]], axis=1)
        vt = jnp.concatenate([r[0].astype(BF16) for r in v_refs[a * npg:(a + 1) * npg]], axis=1)
        s = jnp.dot(qb, kt, preferred_element_type=F32)
        m = jnp.maximum(jnp.max(s, axis=-1, keepdims=True), s_self)
        p = jnp.exp2(s - m)
        p_self = jnp.exp2(s_self - m)
        l = jnp.sum(p, axis=-1, keepdims=True) + p_self
        o = (lax.dot_general(p.astype(BF16), vt, NT_DIMS, preferred_element_type=F32) + p_self * vn_ref[0, 0]) / l
        out = jnp.where(row == a, o, out)
    o_ref[0, 0] = out


def _moba_attend(cache_kt, cache_vt, pages, q_grp, k_new, v_new):
    b, _, npg = pages.shape
    gsz = A_HEADS // A_KV_HEADS

    def pspec(a, i):
        return pl.BlockSpec((1, HEAD_DIM, PAGE_SIZE), lambda bb, kv, pg: (pg[bb, kv * gsz + a, i], kv, 0))

    blk = lambda r: pl.BlockSpec((1, 1, r, HEAD_DIM), lambda bb, kv, pg: (bb, kv, 0, 0))
    page_specs = [pspec(a, i) for a in range(gsz) for i in range(npg)]
    grid_spec = pltpu.PrefetchScalarGridSpec(
        num_scalar_prefetch=1, grid=(b, A_KV_HEADS),
        in_specs=[blk(q_grp.shape[2]), blk(1), blk(1)] + page_specs * 2,
        out_specs=blk(q_grp.shape[2]))
    return pl.pallas_call(
        functools.partial(_moba_attend_body, npg=npg, gsz=gsz), grid_spec=grid_spec,
        out_shape=jax.ShapeDtypeStruct(q_grp.shape, F32),
        compiler_params=_params("parallel", "parallel"),
        name="moba_decode_attend",
    )(pages, q_grp, k_new, v_new, *([cache_kt] * (gsz * npg)), *([cache_vt] * (gsz * npg)))


def _fox_decode_body(pt_ref, q_ref, kn_ref, vn_ref, lfn_ref, *refs, cp):
    k_refs, v_refs, f_refs = refs[:cp], refs[cp:2 * cp], refs[2 * cp:3 * cp]
    o_ref, m_ref, l_ref, acc_ref, run_ref = refs[3 * cp:]
    c = pl.program_id(1)
    q = q_ref[0]

    @pl.when(c == 0)
    def _():
        m_ref[...] = jnp.sum(q * kn_ref[0], axis=-1, keepdims=True)
        l_ref[...] = jnp.ones_like(l_ref)
        acc_ref[...] = jnp.broadcast_to(vn_ref[0], acc_ref.shape)
        run_ref[...] = lfn_ref[0]

    qb = q.astype(BF16)
    r0 = lax.broadcasted_iota(jnp.int32, (PAGE_SIZE, 2 * PAGE_SIZE), 0)
    r1 = lax.broadcasted_iota(jnp.int32, (PAGE_SIZE, 2 * PAGE_SIZE), 1)
    tri = ((r0 > r1) | (r1 >= PAGE_SIZE)).astype(BF16)
    m, l, acc, run = m_ref[...], l_ref[...], acc_ref[...], run_ref[...]
    lf = jnp.concatenate([r[0] for r in f_refs], axis=0)
    d = sum(jnp.dot(part, tri, preferred_element_type=F32) for part in _split3(lf))
    biases = []
    for i in range(cp):
        di = d[i * B_HEADS:(i + 1) * B_HEADS]
        biases.append(run + di[:, :PAGE_SIZE])
        run = run + di[:, PAGE_SIZE:PAGE_SIZE + 1]
    kt = jnp.concatenate([r[0].astype(BF16) for r in k_refs], axis=1)
    vt = jnp.concatenate([r[0].astype(BF16) for r in v_refs], axis=1)
    s = jnp.dot(qb, kt, preferred_element_type=F32) + jnp.concatenate(biases, axis=1) * LOG2E
    m_new = jnp.maximum(m, jnp.max(s, axis=-1, keepdims=True))
    p = jnp.exp2(s - m_new)
    alpha = jnp.exp2(m - m_new)
    l = alpha * l + jnp.sum(p, axis=-1, keepdims=True)
    acc = alpha * acc + lax.dot_general(p.astype(BF16), vt, NT_DIMS, preferred_element_type=F32)
    m_ref[...], l_ref[...], acc_ref[...], run_ref[...] = m_new, l, acc, run

    @pl.when(c == pl.num_programs(1) - 1)
    def _():
        o_ref[0] = acc / l


def _fox_decode(cache_kt, cache_vt, cache_ft, page_table, qbd, k_new, v_new, lf_new):
    b, n_pages = page_table.shape
    cp = _pages_per_step(n_pages)
    assert n_pages % cp == 0
    rev = lambda bb, c, i: n_pages - 1 - (c * cp + i)
    vec = lambda n: pl.BlockSpec((1, 1, n), lambda bb, c, pt: (bb, 0, 0))
    grid_spec = pltpu.PrefetchScalarGridSpec(
        num_scalar_prefetch=1, grid=(b, n_pages // cp),
        in_specs=[pl.BlockSpec((1, B_HEADS, LANES), lambda bb, c, pt: (bb, 0, 0)), vec(LANES), vec(LANES),
                  pl.BlockSpec((1, B_HEADS, 1), lambda bb, c, pt: (bb, 0, 0))]
                 + _page_specs(cp, LANES, PAGE_SIZE, rev) * 2 + _page_specs(cp, B_HEADS, PAGE_SIZE, rev),
        out_specs=pl.BlockSpec((1, B_HEADS, LANES), lambda bb, c, pt: (bb, 0, 0)),
        scratch_shapes=[pltpu.VMEM((B_HEADS, 1), F32), pltpu.VMEM((B_HEADS, 1), F32),
                        pltpu.VMEM((B_HEADS, LANES), F32), pltpu.VMEM((B_HEADS, 1), F32)])
    return pl.pallas_call(
        functools.partial(_fox_decode_body, cp=cp), grid_spec=grid_spec,
        out_shape=jax.ShapeDtypeStruct((b, B_HEADS, LANES), F32),
        compiler_params=_params("parallel", "arbitrary"),
        name="fox_decode",
    )(page_table, qbd, k_new[:, None, :], v_new[:, None, :], lf_new[:, :, None],
      *([cache_kt] * cp), *([cache_vt] * cp), *([cache_ft] * cp))


def _mla_decode_body(pt_ref, ql_ref, qp_ref, cn_ref, pn_ref, *refs, cp):
    c_refs, p_refs = refs[:cp], refs[cp:2 * cp]
    o_ref, m_ref, l_ref, acc_ref = refs[2 * cp:]
    c = pl.program_id(1)
    ql, qp = ql_ref[0], qp_ref[0]

    @pl.when(c == 0)
    def _():
        m_ref[...] = (jnp.sum(ql * cn_ref[0], axis=-1, keepdims=True) + jnp.sum(qp * pn_ref[0], axis=-1, keepdims=True))
        l_ref[...] = jnp.ones_like(l_ref)
        acc_ref[...] = jnp.broadcast_to(cn_ref[0], acc_ref.shape)

    qlb, qpb = ql.astype(BF16), qp.astype(BF16)
    m, l, acc = m_ref[...], l_ref[...], acc_ref[...]
    ck = jnp.concatenate([r[0].astype(BF16) for r in c_refs], axis=0)
    pt = jnp.concatenate([r[0].astype(BF16) for r in p_refs], axis=1)
    s = (lax.dot_general(qlb, ck, NT_DIMS, preferred_element_type=F32)
         + jnp.dot(qpb, pt, preferred_element_type=F32))
    m_new = jnp.maximum(m, jnp.max(s, axis=-1, keepdims=True))
    p = jnp.exp2(s - m_new)
    alpha = jnp.exp2(m - m_new)
    l = alpha * l + jnp.sum(p, axis=-1, keepdims=True)
    acc = alpha * acc + jnp.dot(p.astype(BF16), ck, preferred_element_type=F32)
    m_ref[...], l_ref[...], acc_ref[...] = m_new, l, acc

    @pl.when(c == pl.num_programs(1) - 1)
    def _():
        o_ref[0] = acc / l


def _mla_decode(cache_c, cache_pt, page_table, q_lat, q_pe, c_new, p_new):
    b, n_pages = page_table.shape
    cp = _pages_per_step(n_pages)
    assert n_pages % cp == 0
    fwd = lambda bb, c, i: c * cp + i
    blk = lambda r, n: pl.BlockSpec((1, r, n), lambda bb, c, pt: (bb, 0, 0))
    grid_spec = pltpu.PrefetchScalarGridSpec(
        num_scalar_prefetch=1, grid=(b, n_pages // cp),
        in_specs=[blk(C_HEADS, C_KV_LORA), blk(C_HEADS, C_ROPE), blk(1, C_KV_LORA), blk(1, C_ROPE)]
                 + _page_specs(cp, PAGE_SIZE, C_KV_LORA, fwd) + _page_specs(cp, C_ROPE, PAGE_SIZE, fwd),
        out_specs=blk(C_HEADS, C_KV_LORA),
        scratch_shapes=[pltpu.VMEM((C_HEADS, 1), F32), pltpu.VMEM((C_HEADS, 1), F32), pltpu.VMEM((C_HEADS, C_KV_LORA), F32)])
    return pl.pallas_call(
        functools.partial(_mla_decode_body, cp=cp), grid_spec=grid_spec,
        out_shape=jax.ShapeDtypeStruct((b, C_HEADS, C_KV_LORA), F32),
        compiler_params=_params("parallel", "arbitrary"),
        name="mla_decode",
    )(page_table, q_lat, q_pe, c_new[:, None, :], p_new[:, None, :], *([cache_c] * cp), *([cache_pt] * cp))


def _feature_major(cache):
    pool, page = cache.shape[:2]
    return cache.transpose(0, 2, 3, 1).reshape(pool, -1, page)


def _block_diag_q(q):
    b = q.shape[0]
    qh = q.astype(F32).reshape(b, 2, 4, HEAD_DIM)
    z = jnp.zeros_like(qh)
    return jnp.concatenate([jnp.concatenate([qh[:, 0], z[:, 0]], axis=-1),
                            jnp.concatenate([z[:, 1], qh[:, 1]], axis=-1)], axis=1)


def _diag_halves(o):
    b = o.shape[0]
    return jnp.concatenate([o[:, :4, :HEAD_DIM].reshape(b, 4 * HEAD_DIM), o[:, 4:, HEAD_DIM:].reshape(b, 4 * HEAD_DIM)], axis=-1)


def kernel(x_prompt, x_sample, cache_moba_k, cache_moba_v, cache_fox_k, cache_fox_v, cache_fox_logf, cache_mla_ckv,
           cache_mla_kpe, state_ffn_conv, page_table, norm_mix, norm_ffn, norm_final, w_in_even, b_forget, w_out_even,
           w_dq, g_q, w_uq, w_dkv, g_kv, w_uk, w_uv, w_out_odd, w_gate, w_up, conv_w, conv_b, w_down):
    bp, sp, d = x_prompt.shape
    bs, ts, _ = x_sample.shape
    assert bp == 1 and ts == 1
    n_pages = page_table.shape[1]
    past = n_pages * PAGE_SIZE
    assert past % MOBA_BLOCK == 0 and past // MOBA_BLOCK >= MOBA_TOPK and sp % MOBA_BLOCK == 0
    depth = norm_mix.shape[0]
    pos_p = jnp.arange(sp, dtype=jnp.int32)
    pos_s = jnp.full((bs,), past, jnp.int32)
    xp = x_prompt[0]
    xs = x_sample[:, 0]
    outs_p = {k: [] for k in ("mk", "mv", "fk", "fv", "fl", "ck", "kp", "cv")}
    outs_s = {k: [] for k in ("mk", "mv", "fk", "fv", "fl", "ck", "kp", "cv")}
    row2 = lambda v: v.reshape(1, -1)

    for layer in range(depth):
        if layer % 2 == 0:
            e = layer // 2
            w = w_in_even[e]
            w_cat = jnp.concatenate([w, jnp.zeros((d, LANES - B_HEADS), F32)], axis=1).astype(BF16)
            bf_pad = jnp.concatenate([b_forget[e], jnp.zeros((LANES - B_HEADS,), F32)]).reshape(1, LANES)
            g = row2(norm_mix[layer])
            w_out = w_out_even[e].astype(BF16)
            qa, ka, va, qb, kb, vb, lf, chi, cmid, clo = _even_proj(xp, g, w_cat, bf_pad, pos_p)
            nb = sp // MOBA_BLOCK
            bias = _moba_gate(qa, _block_mean(ka))
            qpa = jnp.concatenate([qa.reshape(sp, A_HEADS, HEAD_DIM), bias.reshape(sp, A_HEADS, HEAD_DIM)], axis=-1)
            onehot = (pos_p[:, None] // MOBA_BLOCK == jnp.arange(HEAD_DIM)[None, :]).astype(BF16)
            kpa = jnp.concatenate([ka.astype(BF16).reshape(sp, 2, HEAD_DIM).transpose(1, 0, 2),
                                   jnp.broadcast_to(onehot[None], (2, sp, HEAD_DIM))], axis=-1)
            oa = _flash(qpa.reshape(sp, A_HEADS * LANES), kpa, va.astype(BF16).T, g=4, tq=512, tk=512)
            cparts = jnp.stack([chi, cmid, clo], axis=-1)
            sel = jnp.tile(jnp.repeat(jnp.eye(4, dtype=BF16), 3, axis=1), (2, 1))
            aug_q = jnp.concatenate([jnp.broadcast_to(sel[None], (sp, B_HEADS, 12)), cparts,
                                     jnp.zeros((sp, B_HEADS, HEAD_DIM - 15), BF16)], axis=-1)
            qpb = jnp.concatenate([qb.reshape(sp, B_HEADS, HEAD_DIM), aug_q], axis=-1)
            aug_k = jnp.concatenate([(-cparts).reshape(sp, 2, 12), jnp.ones((sp, 2, 3), BF16),
                                     jnp.zeros((sp, 2, HEAD_DIM - 15), BF16)], axis=-1)
            kpb = jnp.concatenate([kb.astype(BF16).reshape(sp, 2, HEAD_DIM), aug_k], axis=-1).transpose(1, 0, 2)
            ob = _flash(qpb.reshape(sp, B_HEADS * LANES), kpb, vb.astype(BF16).T, g=4, tq=512, tk=512)
            xp = _out_proj(xp, [oa, ob], [w_out[:A_HEADS * HEAD_DIM], w_out[A_HEADS * HEAD_DIM:]])
            for key, val in zip(("mk", "mv", "fk", "fv"), (ka, va, kb, vb)):
                outs_p[key].append(val.reshape(1, sp, 2, HEAD_DIM))
            outs_p["fl"].append(lf.reshape(1, sp, B_HEADS))
            qa, ka, va, qb, kb, vb, lf, _, _, _ = _even_proj(xs, g, w_cat, bf_pad, pos_s)
            ck, cv = _feature_major(cache_moba_k[e]), _feature_major(cache_moba_v[e])
            qbd = _block_diag_q(qa)
            picks = _moba_pick(ck, page_table, qbd)[:, :, :MOBA_TOPK]
            per = MOBA_BLOCK // PAGE_SIZE
            logical = (picks[..., None] * per + jnp.arange(per, dtype=jnp.int32)).reshape(bs, A_HEADS, MOBA_TOPK * per)
            pages = jnp.take_along_axis(page_table[:, None, :], logical, axis=2)
            gsz = A_HEADS // A_KV_HEADS
            q_grp = qa.astype(F32).reshape(bs, A_KV_HEADS, gsz, HEAD_DIM)
            q_grp = jnp.concatenate([q_grp, jnp.zeros((bs, A_KV_HEADS, 8 - gsz, HEAD_DIM), F32)], axis=2)
            oa = _moba_attend(ck, cv, pages, q_grp, ka.reshape(bs, A_KV_HEADS, 1, HEAD_DIM),
                              va.reshape(bs, A_KV_HEADS, 1, HEAD_DIM))
            oa = oa[:, :, :gsz].reshape(bs, A_HEADS * HEAD_DIM).astype(BF16)
            ob = _fox_decode(_feature_major(cache_fox_k[e]), _feature_major(cache_fox_v[e]),
                             cache_fox_logf[e].transpose(0, 2, 1), page_table, _block_diag_q(qb), kb, vb, lf)
            ob = _diag_halves(ob).astype(BF16)
            xs = _out_proj(xs, [oa, ob], [w_out[:A_HEADS * HEAD_DIM], w_out[A_HEADS * HEAD_DIM:]])
            for key, val in zip(("mk", "mv", "fk", "fv"), (ka, va, kb, vb)):
                outs_s[key].append(val.reshape(bs, 1, 2, HEAD_DIM))
            outs_s["fl"].append(lf.reshape(bs, 1, B_HEADS))
        else:
            o = layer // 2
            dq = C_NOPE + C_ROPE
            wq = w_uq[o].reshape(C_Q_LORA, C_HEADS, dq)
            wuq_slots = jnp.concatenate([wq, jnp.zeros((C_Q_LORA, C_HEADS, LANES - dq), F32)], axis=-1)
            wuq_slots = wuq_slots.reshape(C_Q_LORA, C_HEADS * LANES).astype(BF16)
            wdkv_pad = jnp.concatenate([w_dkv[o], jnp.zeros((d, LANES - C_ROPE), F32)], axis=1).astype(BF16)
            wk_top = jnp.concatenate([w_uk[o], jnp.zeros((C_KV_LORA, C_HEADS, LANES - C_NOPE), F32)], axis=-1)
            place = jnp.concatenate([jnp.zeros((C_ROPE, C_NOPE), F32), jnp.eye(C_ROPE, dtype=F32),
                                     jnp.zeros((C_ROPE, LANES - dq), F32)], axis=1)
            wk_mid = jnp.broadcast_to(place[:, None, :], (C_ROPE, C_HEADS, LANES))
            wk_aug = jnp.concatenate([wk_top, wk_mid, jnp.zeros((LANES - C_ROPE, C_HEADS, LANES), F32)], axis=0)
            wk_aug = wk_aug.reshape(C_KV_LORA + LANES, C_HEADS * LANES).astype(BF16)
            wv = w_uv[o].reshape(C_KV_LORA, C_HEADS * C_V).astype(BF16)
            ws = (row2(norm_mix[layer]), w_dq[o].astype(BF16), row2(g_q[o]), wuq_slots, wdkv_pad, row2(g_kv[o]), wk_aug, wv)
            w_out = w_out_odd[o].astype(BF16)
            qp, ckv, kpe, kp, v = _mla_proj(xp, *ws, pos_p)
            oc = _flash(qp, kp.reshape(sp, C_HEADS, LANES).transpose(1, 0, 2), v.T, g=1, tq=1024, tk=512)
            xp = _out_proj(xp, [oc], [w_out])
            outs_p["ck"].append(ckv.reshape(1, sp, C_KV_LORA))
            outs_p["kp"].append(kpe.reshape(1, sp, C_ROPE))
            qp, ckv, kpe, _, _ = _mla_proj(xs, *ws, pos_s)
            qh = qp.reshape(bs, C_HEADS, LANES).transpose(1, 0, 2)
            q_lat = _head_mm(qh[:, :, :C_NOPE], w_uk[o].transpose(1, 2, 0).astype(BF16))
            o_lat = _mla_decode(cache_mla_ckv[o], cache_mla_kpe[o].transpose(0, 2, 1), page_table, q_lat.transpose(1, 0, 2),
                                qh[:, :, C_NOPE:dq].transpose(1, 0, 2).astype(F32), ckv, kpe)
            oc = _head_mm(o_lat.transpose(1, 0, 2), w_uv[o].transpose(1, 0, 2).astype(BF16))
            xs = _out_proj(xs, [oc.transpose(1, 0, 2).reshape(bs, C_HEADS * C_V).astype(BF16)], [w_out])
            outs_s["ck"].append(ckv.reshape(bs, 1, C_KV_LORA))
            outs_s["kp"].append(kpe.reshape(bs, 1, C_ROPE))
        ffn_w = (row2(norm_ffn[layer]), w_gate[layer].astype(BF16), w_up[layer].astype(BF16), conv_w[layer],
                 row2(conv_b[layer]), w_down[layer].astype(BF16))
        xp, st = _ffn(xp, *ffn_w)
        outs_p["cv"].append(st[8 - (CONV_W - 1):].reshape(1, CONV_W - 1, -1))
        xs, gate_rows = _ffn(xs, *ffn_w, state=state_ffn_conv[layer])
        outs_s["cv"].append(jnp.stack([state_ffn_conv[layer][:, 1], gate_rows], axis=1))

    gf = row2(norm_final)
    order = ("mk", "mv", "fk", "fv", "fl", "ck", "kp", "cv")
    return ((_final_norm(xp, gf).reshape(bp, sp, d), _final_norm(xs, gf).reshape(bs, ts, d))
            + tuple(jnp.stack(outs_p[k]) for k in order) + tuple(jnp.stack(outs_s[k]) for k in order))
```

```python
import functools
import math

import numpy as np
import jax
import jax.numpy as jnp
from jax import lax
from jax.experimental import pallas as pl
from jax.experimental.pallas import tpu as pltpu

F32 = jnp.float32
BF16 = jnp.bfloat16

HEAD_DIM = 64
A_HEADS, A_KV_HEADS = 8, 2
B_HEADS, B_KV_HEADS = 8, 2
C_HEADS, C_Q_LORA, C_KV_LORA, C_NOPE, C_ROPE, C_V = 16, 512, 256, 64, 32, 64
MOBA_BLOCK, MOBA_TOPK = 256, 3
PAGE_SIZE = 128
CONV_W = 3
ROPE_THETA = 10000.0
EPS = 1e-6
LANES = 128
NEG = -1e30
LOG2E = math.log2(math.e)
VMEM_LIMIT = 56 * 1024 * 1024
HIGHEST = lax.Precision.HIGHEST
NT_DIMS = (((1,), (1,)), ((), ()))
TN_DIMS = (((0,), (0,)), ((), ()))


def _params(*sem):
    return pltpu.CompilerParams(dimension_semantics=sem, vmem_limit_bytes=VMEM_LIMIT)


def _rms(x, g):
    return x * lax.rsqrt(jnp.mean(x * x, axis=-1, keepdims=True) + EPS) * g


def _rope_tables(pos, kind, freq, half, d):
    inv = ROPE_THETA ** (-jnp.arange(half, dtype=F32) * (2.0 / d))
    ang = pos.astype(F32)[:, None] * inv[None, :]
    cos, sin = jnp.cos(ang)[:, freq], jnp.sin(ang)[:, freq]
    kind = jnp.asarray(kind)[None, :]
    return (jnp.where(kind == 0, 1.0, cos).astype(F32), jnp.where(kind == 1, -sin, 0.0).astype(F32),
            jnp.where(kind == 2, sin, 0.0).astype(F32))


def _rope_apply(z, cos, sa, sb, half):
    outs = []
    for c in range(z.shape[1] // LANES):
        x = z[:, c * LANES:(c + 1) * LANES]
        outs.append(x * cos + pltpu.roll(x, LANES - half, 1) * sa + pltpu.roll(x, half, 1) * sb)
    return outs[0] if len(outs) == 1 else jnp.concatenate(outs, axis=1)


def _split3(x):
    hi = x.astype(BF16)
    r = x - hi.astype(F32)
    mid = r.astype(BF16)
    lo = (r - mid.astype(F32)).astype(BF16)
    return hi, mid, lo


def _row_tile(t, want):
    return want if t % want == 0 else t


def _even_proj_body(x_ref, g_ref, w_ref, bf_ref, cos_ref, sa_ref, sb_ref,
                    qa_ref, ka_ref, va_ref, qb_ref, kb_ref, vb_ref, lf_ref, chi_ref, cmid_ref, clo_ref, carry_ref):
    @pl.when(pl.program_id(0) == 0)
    def _():
        carry_ref[...] = jnp.zeros_like(carry_ref)

    tm = x_ref.shape[0]
    h = _rms(x_ref[...], g_ref[...]).astype(BF16)
    z = jnp.dot(h, w_ref[...], preferred_element_type=F32)
    cos, sa, sb = cos_ref[...], sa_ref[...], sb_ref[...]
    half = HEAD_DIM // 2
    scale = HEAD_DIM ** -0.5 * LOG2E
    o = 0
    qa_ref[...] = (_rope_apply(z[:, o:o + 512], cos, sa, sb, half) * scale).astype(BF16); o += 512
    ka_ref[...] = _rope_apply(z[:, o:o + 128], cos, sa, sb, half); o += 128
    va_ref[...] = z[:, o:o + 128]; o += 128
    qb_ref[...] = (z[:, o:o + 512] * scale).astype(BF16); o += 512
    kb_ref[...] = z[:, o:o + 128]; o += 128
    vb_ref[...] = z[:, o:o + 128]; o += 128
    fl = z[:, o:o + 128] + bf_ref[...]
    lf = jnp.minimum(fl, 0.0) - jnp.log1p(jnp.exp(-jnp.abs(fl)))
    lf_ref[...] = lf[:, :B_HEADS]
    tri = (lax.broadcasted_iota(jnp.int32, (tm, tm), 0) >= lax.broadcasted_iota(jnp.int32, (tm, tm), 1)).astype(F32)
    cum = jnp.dot(tri, lf, preferred_element_type=F32, precision=HIGHEST) + carry_ref[...]
    carry_ref[...] = cum[tm - 1:tm, :]
    hi, mid, lo = _split3(cum[:, :B_HEADS] * LOG2E)
    chi_ref[...] = hi
    cmid_ref[...] = mid
    clo_ref[...] = lo


def _even_proj(x, g, w_cat, bf_pad, pos):
    t, d = x.shape
    tm = _row_tile(t, 512)
    lane = np.arange(LANES) % HEAD_DIM
    cos, sa, sb = _rope_tables(pos, np.where(lane < 32, 1, 2), lane % 32, HEAD_DIM // 2, HEAD_DIM)
    row = lambda n: pl.BlockSpec((tm, n), lambda i: (i, 0))
    full = lambda a: pl.BlockSpec(a.shape, lambda i: (0, 0))
    outs = [(512, BF16), (128, F32), (128, F32), (512, BF16), (128, F32), (128, F32), (B_HEADS, F32),
            (B_HEADS, BF16), (B_HEADS, BF16), (B_HEADS, BF16)]
    return pl.pallas_call(
        _even_proj_body,
        grid=(t // tm,),
        in_specs=[row(d), full(g), full(w_cat), full(bf_pad), row(LANES), row(LANES), row(LANES)],
        out_specs=[row(n) for n, _ in outs],
        out_shape=[jax.ShapeDtypeStruct((t, n), dt) for n, dt in outs],
        scratch_shapes=[pltpu.VMEM((1, LANES), F32)],
        compiler_params=_params("arbitrary"),
        name="even_proj",
    )(x, g, w_cat, bf_pad, cos, sa, sb)


def _block_mean_body(k_ref, o_ref):
    nb = o_ref.shape[0]
    o_ref[...] = jnp.mean(k_ref[...].reshape(nb, MOBA_BLOCK, LANES), axis=1)


def _block_mean(k):
    t = k.shape[0]
    nb = t // MOBA_BLOCK
    return pl.pallas_call(
        _block_mean_body,
        out_shape=jax.ShapeDtypeStruct((nb, LANES), F32),
        compiler_params=_params(),
        name="moba_block_mean",
    )(k)


def _top3_bias(gate, n_valid, lane):
    cand = jnp.where(lane < n_valid, gate, -jnp.inf)
    keep = lane >= n_valid
    for _ in range(MOBA_TOPK):
        mx = jnp.max(cand, axis=-1, keepdims=True)
        first = jnp.min(jnp.where(cand == mx, lane, gate.shape[-1]), axis=-1, keepdims=True)
        hit = (lane == first) & (mx > -jnp.inf)
        keep = keep | hit
        cand = jnp.where(hit, -jnp.inf, cand)
    return jnp.where(keep, 0.0, NEG)


def _moba_gate_body(q_ref, km_ref, o_ref):
    tm = q_ref.shape[0]
    nb = km_ref.shape[0]
    g = A_HEADS // A_KV_HEADS
    pos = pl.program_id(0) * tm + lax.broadcasted_iota(jnp.int32, (tm, 1), 0)
    own = pos // MOBA_BLOCK
    lane = lax.broadcasted_iota(jnp.int32, (tm, nb), 1)
    for h in range(A_HEADS):
        q = q_ref[:, h * HEAD_DIM:(h + 1) * HEAD_DIM].astype(F32)
        km = km_ref[:, (h // g) * HEAD_DIM:(h // g + 1) * HEAD_DIM]
        gate = lax.dot_general(q, km, NT_DIMS, preferred_element_type=F32, precision=HIGHEST)
        bias = _top3_bias(gate, own, lane)
        if nb < HEAD_DIM:
            bias = jnp.concatenate([bias, jnp.zeros((tm, HEAD_DIM - nb), F32)], axis=1)
        o_ref[:, h * HEAD_DIM:(h + 1) * HEAD_DIM] = bias.astype(BF16)


def _moba_gate(q, kmean):
    t = q.shape[0]
    tm = _row_tile(t, 512)
    assert kmean.shape[0] <= HEAD_DIM, "one bias lane per key block"
    return pl.pallas_call(
        _moba_gate_body,
        grid=(t // tm,),
        in_specs=[pl.BlockSpec((tm, q.shape[1]), lambda i: (i, 0)), pl.BlockSpec(kmean.shape, lambda i: (0, 0))],
        out_specs=pl.BlockSpec((tm, A_HEADS * HEAD_DIM), lambda i: (i, 0)),
        out_shape=jax.ShapeDtypeStruct((t, A_HEADS * HEAD_DIM), BF16),
        compiler_params=_params("parallel"),
        name="moba_gate",
    )(q, kmean)


FLASH_COLS = 512


def _flash_body(q_ref, k_ref, vt_ref, o_ref, *, g, tq, tk, tpi):
    i = pl.program_id(1)
    r = g * tq
    n_full = (i * tq) // tk
    n_mask = tq // tk
    ncb = r // FLASH_COLS
    qs = [jnp.concatenate([q_ref[:, (s * g + a) * LANES:(s * g + a + 1) * LANES] for a in range(g)], axis=0)
          if g > 1 else q_ref[:, s * LANES:(s + 1) * LANES] for s in range(2)]

    def tiles(j0, carry, masked, nt):
        state = list(carry)
        chains = [(u, s, cb) for u in range(nt) for s in range(2) for cb in range(ncb)]
        starts = [pl.multiple_of((j0 + u) * tk, tk) for u in range(nt)]
        scores, probs = {}, {}

        def qk(c):
            u, s, cb = chains[c]
            k = k_ref[s, pl.ds(starts[u], tk), :]
            q = qs[s][cb * FLASH_COLS:(cb + 1) * FLASH_COLS]
            scores[c] = lax.dot_general(k, q, NT_DIMS, preferred_element_type=F32)

        def softmax(c):
            u, s, cb = chains[c]
            m, l, acc = state[s * ncb + cb]
            x = scores.pop(c)
            if masked:
                kpos = starts[u] + lax.broadcasted_iota(jnp.int32, x.shape, 0)
                qpos = i * tq + (cb * FLASH_COLS + lax.broadcasted_iota(jnp.int32, x.shape, 1)) % tq
                x = jnp.where(kpos <= qpos, x, NEG)
            m_new = jnp.maximum(m, jnp.max(x, axis=0, keepdims=True))
            p = jnp.exp2(x - m_new)
            alpha = jnp.exp2(m - m_new)
            probs[c] = (alpha, p.astype(BF16))
            state[s * ncb + cb] = (m_new, alpha * l + jnp.sum(p, axis=0, keepdims=True), acc)

        def pv(c):
            u, s, cb = chains[c]
            alpha, p = probs.pop(c)
            m, l, acc = state[s * ncb + cb]
            vt = vt_ref[s * HEAD_DIM:(s + 1) * HEAD_DIM, pl.ds(starts[u], tk)]
            state[s * ncb + cb] = (m, l, alpha * acc + jnp.dot(vt, p, preferred_element_type=F32))

        n = len(chains)
        for t in range(n + 2):
            if t < n:
                qk(t)
            if 0 <= t - 1 < n:
                softmax(t - 1)
            if 0 <= t - 2 < n:
                pv(t - 2)
        return tuple(state)

    init = tuple((jnp.full((1, FLASH_COLS), NEG, F32), jnp.zeros((1, FLASH_COLS), F32),
                  jnp.zeros((HEAD_DIM, FLASH_COLS), F32)) for _ in range(2 * ncb))
    carry = lax.fori_loop(0, n_full // tpi, lambda j, c: tiles(j * tpi, c, False, tpi), init)
    carry = tiles(n_full, carry, True, n_mask)
    for s in range(2):
        o = jnp.concatenate([carry[s * ncb + cb][2] / carry[s * ncb + cb][1] for cb in range(ncb)], axis=1)
        for a in range(g):
            col = (s * g + a) * HEAD_DIM
            o_ref[:, col:col + HEAD_DIM] = o[:, a * tq:(a + 1) * tq].T.astype(BF16)


def _flash(qp, kp, vt, *, g, tq, tk):
    t = qp.shape[0]
    nc = kp.shape[0] // 2
    tq, tk = min(tq, t), min(tk, t)
    tpi = tq // tk
    assert t % tq == 0 and tq % tk == 0 and (g * tq) % FLASH_COLS == 0
    return pl.pallas_call(
        functools.partial(_flash_body, g=g, tq=tq, tk=tk, tpi=tpi),
        grid=(nc, t // tq),
        in_specs=[pl.BlockSpec((tq, 2 * g * LANES), lambda c, i: (i, c)),
                  pl.BlockSpec((2, t, LANES), lambda c, i: (c, 0, 0)),
                  pl.BlockSpec((LANES, t), lambda c, i: (c, 0))],
        out_specs=pl.BlockSpec((tq, 2 * g * HEAD_DIM), lambda c, i: (i, c)),
        out_shape=jax.ShapeDtypeStruct((t, nc * 2 * g * HEAD_DIM), BF16),
        compiler_params=_params("parallel", "arbitrary"),
        name="flash_attention",
    )(qp, kp, vt)


def _out_proj_body(*refs, n):
    x_ref, o_ref = refs[0], refs[-1]
    acc = x_ref[...]
    for a_ref, w_ref in zip(refs[1:1 + n], refs[1 + n:1 + 2 * n]):
        acc = acc + jnp.dot(a_ref[...], w_ref[...], preferred_element_type=F32)
    o_ref[...] = acc


def _out_proj(x, acts, ws):
    t, d = x.shape
    tm = _row_tile(t, 512)
    return pl.pallas_call(
        functools.partial(_out_proj_body, n=len(acts)),
        grid=(t // tm,),
        in_specs=[pl.BlockSpec((tm, d), lambda i: (i, 0))]
                 + [pl.BlockSpec((tm, a.shape[1]), lambda i: (i, 0)) for a in acts]
                 + [pl.BlockSpec(w.shape, lambda i: (0, 0)) for w in ws],
        out_specs=pl.BlockSpec((tm, d), lambda i: (i, 0)),
        out_shape=jax.ShapeDtypeStruct((t, d), F32),
        compiler_params=_params("parallel"),
        name="out_proj",
    )(x, *acts, *ws)


FF_CHUNK = 256


def _ffn_body(*refs, seq):
    if seq:
        x_ref, gn_ref, wg_ref, wu_ref, cw_ref, cb_ref, wd_ref, y_ref, st_ref, carry_ref = refs
    else:
        x_ref, gn_ref, wg_ref, wu_ref, cw_ref, cb_ref, wd_ref, s0_ref, s1_ref, y_ref, st_ref = refs
    tm = x_ref.shape[0]
    ff = wg_ref.shape[1]
    if seq:
        @pl.when(pl.program_id(0) == 0)
        def _():
            carry_ref[...] = jnp.zeros_like(carry_ref)
        row = lax.broadcasted_iota(jnp.int32, (tm, FF_CHUNK), 0)

    x = x_ref[...]
    h = _rms(x, gn_ref[...]).astype(BF16)
    acc = x
    for c in range(ff // FF_CHUNK):
        sl = slice(c * FF_CHUNK, (c + 1) * FF_CHUNK)
        gch = jnp.dot(h, wg_ref[:, sl], preferred_element_type=F32)
        uch = jnp.dot(h, wu_ref[:, sl], preferred_element_type=F32)
        if seq:
            prev = carry_ref[:, sl]
            g1 = jnp.where(row == 0, prev[7:8, :], pltpu.roll(gch, 1, 0))
            g2 = jnp.where(row == 0, prev[6:7, :], jnp.where(row == 1, prev[7:8, :], pltpu.roll(gch, 2, 0)))
            carry_ref[:, sl] = gch[tm - 8:tm, :]
            st_ref[:, sl] = gch[tm - 8:tm, :]
        else:
            g1, g2 = s1_ref[:, sl], s0_ref[:, sl]
            st_ref[:, sl] = gch
        gc = cb_ref[:, sl] + cw_ref[0:1, sl] * g2 + cw_ref[1:2, sl] * g1 + cw_ref[2:3, sl] * gch
        act = gc / (1.0 + jnp.exp(-gc)) * uch
        acc = acc + jnp.dot(act.astype(BF16), wd_ref[sl, :], preferred_element_type=F32)
    y_ref[...] = acc


def _ffn(x, gn, wg, wu, cw, cb, wd, state=None):
    t, d = x.shape
    ff = wg.shape[1]
    assert ff % FF_CHUNK == 0
    seq = state is None
    tm = _row_tile(t, 512) if seq else t
    assert tm >= 8
    row = lambda n: pl.BlockSpec((tm, n), lambda i: (i, 0))
    full = lambda a: pl.BlockSpec(a.shape, lambda i: (0, 0))
    ins = [x, gn, wg, wu, cw, cb, wd]
    in_specs = [row(d)] + [full(a) for a in ins[1:]]
    if seq:
        st_shape, st_spec = (8, ff), pl.BlockSpec((8, ff), lambda i: (0, 0))
        scratch = [pltpu.VMEM((8, ff), F32)]
    else:
        ins += [state[:, 0], state[:, 1]]
        in_specs += [row(ff), row(ff)]
        st_shape, st_spec = (t, ff), row(ff)
        scratch = []
    return pl.pallas_call(
        functools.partial(_ffn_body, seq=seq),
        grid=(t // tm,),
        in_specs=in_specs,
        out_specs=[row(d), st_spec],
        out_shape=[jax.ShapeDtypeStruct((t, d), F32), jax.ShapeDtypeStruct(st_shape, F32)],
        scratch_shapes=scratch,
        compiler_params=_params("arbitrary"),
        name="conv_ffn",
    )(*ins)


def _mla_proj_body(x_ref, g_ref, wdq_ref, gq_ref, wuq_ref, wdkv_ref, gkv_ref, wk_ref, wv_ref,
                   cq_ref, sq_ref, tq_ref, ck_ref, sk_ref, tk_ref,
                   qp_ref, ckv_ref, kpe_ref, kp_ref, v_ref):
    h = _rms(x_ref[...], g_ref[...]).astype(BF16)
    cq = _rms(jnp.dot(h, wdq_ref[...], preferred_element_type=F32), gq_ref[...]).astype(BF16)
    q = jnp.dot(cq, wuq_ref[...], preferred_element_type=F32)
    scale = (C_NOPE + C_ROPE) ** -0.5 * LOG2E
    qp_ref[...] = (_rope_apply(q, cq_ref[...], sq_ref[...], tq_ref[...], C_ROPE // 2) * scale).astype(BF16)
    kv = jnp.dot(h, wdkv_ref[...], preferred_element_type=F32)
    ckv = _rms(kv[:, :C_KV_LORA], gkv_ref[...])
    kpe = _rope_apply(kv[:, C_KV_LORA:], ck_ref[...], sk_ref[...], tk_ref[...], C_ROPE // 2)
    ckv_ref[...] = ckv
    kpe_ref[...] = kpe[:, :C_ROPE]
    kin = jnp.concatenate([ckv, kpe], axis=1).astype(BF16)
    kp_ref[...] = jnp.dot(kin, wk_ref[...], preferred_element_type=F32).astype(BF16)
    v_ref[...] = jnp.dot(kin[:, :C_KV_LORA], wv_ref[...], preferred_element_type=F32).astype(BF16)


def _mla_proj(x, g, wdq, gq, wuq_slots, wdkv_pad, gkv, wk_aug, wv, pos):
    t, d = x.shape
    tm = _row_tile(t, 512)
    lane = np.arange(LANES)
    in_q = (lane >= C_NOPE) & (lane < C_NOPE + C_ROPE)
    dq = lane - C_NOPE
    tabs_q = _rope_tables(pos, np.where(in_q, np.where(dq < C_ROPE // 2, 1, 2), 0),
                          np.where(in_q, dq % (C_ROPE // 2), 0), C_ROPE // 2, C_ROPE)
    in_k = lane < C_ROPE
    tabs_k = _rope_tables(pos, np.where(in_k, np.where(lane < C_ROPE // 2, 1, 2), 0),
                          np.where(in_k, lane % (C_ROPE // 2), 0), C_ROPE // 2, C_ROPE)
    row = lambda n: pl.BlockSpec((tm, n), lambda i: (i, 0))
    full = lambda a: pl.BlockSpec(a.shape, lambda i: (0, 0))
    ws = [g, wdq, gq, wuq_slots, wdkv_pad, gkv, wk_aug, wv]
    outs = [(C_HEADS * LANES, BF16), (C_KV_LORA, F32), (C_ROPE, F32), (C_HEADS * LANES, BF16), (C_HEADS * C_V, BF16)]
    return pl.pallas_call(
        _mla_proj_body,
        grid=(t // tm,),
        in_specs=[row(d)] + [full(a) for a in ws] + [row(LANES)] * 6,
        out_specs=[row(n) for n, _ in outs],
        out_shape=[jax.ShapeDtypeStruct((t, n), dt) for n, dt in outs],
        compiler_params=_params("parallel"),
        name="mla_proj",
    )(x, *ws, *tabs_q, *tabs_k)


def _head_mm_body(x_ref, w_ref, o_ref):
    o_ref[0] = jnp.dot(x_ref[0].astype(BF16), w_ref[0], preferred_element_type=F32)


def _head_mm(x, w):
    hh, b, k = x.shape
    n = w.shape[2]
    return pl.pallas_call(
        _head_mm_body,
        grid=(hh,),
        in_specs=[pl.BlockSpec((1, b, k), lambda i: (i, 0, 0)), pl.BlockSpec((1, k, n), lambda i: (i, 0, 0))],
        out_specs=pl.BlockSpec((1, b, n), lambda i: (i, 0, 0)),
        out_shape=jax.ShapeDtypeStruct((hh, b, n), F32),
        compiler_params=_params("parallel"),
        name="head_matmul",
    )(x, w)


def _norm_body(x_ref, g_ref, o_ref):
    o_ref[...] = _rms(x_ref[...], g_ref[...])


def _final_norm(x, g):
    t, d = x.shape
    tm = _row_tile(t, 1024)
    return pl.pallas_call(
        _norm_body,
        grid=(t // tm,),
        in_specs=[pl.BlockSpec((tm, d), lambda i: (i, 0)), pl.BlockSpec(g.shape, lambda i: (0, 0))],
        out_specs=pl.BlockSpec((tm, d), lambda i: (i, 0)),
        out_shape=jax.ShapeDtypeStruct((t, d), F32),
        compiler_params=_params("parallel"),
        name="final_norm",
    )(x, g)


class _PagedInput:
    def __init__(self, hbm, buf, sem, window):
        self.hbm, self.buf, self.sem, self.window = hbm, buf, sem, window

    def copy(self, page, slot, i):
        return pltpu.make_async_copy(self.hbm.at[page], self.window(self.buf.at[slot], i), self.sem.at[slot])


def _paged_chunks(pt_ref, inputs, n_pages, cp, page_of, compute):
    b = pl.program_id(0)
    n_chunks = n_pages // cp
    assert n_chunks % 2 == 0, "chunk c uses buffer c % 2, so every sequence must start on buffer 0"

    def chunk_copies(bb, c, slot):
        return [inp.copy(pt_ref[bb, page_of(c * cp + i)], slot, i) for i in range(cp) for inp in inputs]

    @pl.when(b == 0)
    def _():
        for cpy in chunk_copies(0, 0, 0):
            cpy.start()

    for c in range(n_chunks):
        slot = c % 2
        if c + 1 < n_chunks:
            for cpy in chunk_copies(b, c + 1, 1 - slot):
                cpy.start()
        else:
            @pl.when(b + 1 < pl.num_programs(0))
            def _():
                for cpy in chunk_copies(b + 1, 0, 0):
                    cpy.start()
        for cpy in chunk_copies(b, c, slot):
            cpy.wait()
        compute(c, slot)


def _pages_per_step(n_pages):
    return min(16, n_pages // 2)


def _top3_ids(cand, lane, out_lane):
    res = jnp.zeros(out_lane.shape, jnp.int32)
    for r in range(MOBA_TOPK):
        mx = jnp.max(cand, axis=-1, keepdims=True)
        first = jnp.min(jnp.where(cand == mx, lane, cand.shape[-1]), axis=-1, keepdims=True)
        res = jnp.where(out_lane == r, first, res)
        cand = jnp.where(lane == first, -jnp.inf, cand)
    return res


def _moba_pick_body(pt_ref, q_ref, k_hbm, o_ref, kbuf, ksem, *, cp, n_pages):
    per = MOBA_BLOCK // PAGE_SIZE
    qb = q_ref[0].astype(BF16)
    lane = lax.broadcasted_iota(jnp.int32, (A_HEADS, LANES), 1)
    state = [jnp.full((A_HEADS, LANES), -jnp.inf, F32)]

    def compute(c, slot):
        s = jnp.dot(qb, kbuf[slot].astype(BF16), preferred_element_type=F32)
        gate = state[0]
        for n in range(cp // per):
            val = jnp.sum(s[:, n * MOBA_BLOCK:(n + 1) * MOBA_BLOCK], axis=-1, keepdims=True) * (1.0 / MOBA_BLOCK)
            gate = jnp.where(lane == c * (cp // per) + n, val, gate)
        state[0] = gate

    inputs = [_PagedInput(k_hbm, kbuf, ksem, lambda buf, i: buf.at[:, pl.ds(i * PAGE_SIZE, PAGE_SIZE)])]
    _paged_chunks(pt_ref, inputs, n_pages, cp, lambda p: p, compute)
    o_ref[0] = _top3_ids(state[0], lane, lane)


def _moba_pick(cache_kt, page_table, qbd):
    b, n_pages = page_table.shape
    cp = _pages_per_step(n_pages)
    per = MOBA_BLOCK // PAGE_SIZE
    assert n_pages % cp == 0 and cp % per == 0 and n_pages // per <= LANES
    grid_spec = pltpu.PrefetchScalarGridSpec(
        num_scalar_prefetch=1, grid=(b,),
        in_specs=[pl.BlockSpec((1, A_HEADS, LANES), lambda bb, pt: (bb, 0, 0)), pl.BlockSpec(memory_space=pl.ANY)],
        out_specs=pl.BlockSpec((1, A_HEADS, LANES), lambda bb, pt: (bb, 0, 0)),
        scratch_shapes=[pltpu.VMEM((2, LANES, cp * PAGE_SIZE), F32), pltpu.SemaphoreType.DMA((2,))])
    return pl.pallas_call(
        functools.partial(_moba_pick_body, cp=cp, n_pages=n_pages), grid_spec=grid_spec,
        out_shape=jax.ShapeDtypeStruct((b, A_HEADS, LANES), jnp.int32),
        compiler_params=_params("arbitrary"),
        name="moba_decode_pick",
    )(page_table, qbd, cache_kt)


def _moba_attend_body(pg_ref, q_ref, kn_ref, vn_ref, *refs, npg, gsz):
    n = gsz * npg
    k_refs, v_refs, o_ref = refs[:n], refs[n:2 * n], refs[2 * n]
    q = q_ref[0, 0]
    qb = q.astype(BF16)
    s_self = jnp.sum(q * kn_ref[0, 0], axis=-1, keepdims=True)
    row = lax.broadcasted_iota(jnp.int32, (q.shape[0], HEAD_DIM), 0)
    out = jnp.zeros((q.shape[0], HEAD_DIM), F32)
    for a in range(gsz):
        kt = jnp.concatenate([r[0].astype(BF16) for r in k_refs[a * npg:(a + 1) * npg]], axis=1)
        vt = jnp.concatenate([r[0].astype(BF16) for r in v_refs[a * npg:(a + 1) * npg]], axis=1)
        s = jnp.dot(qb, kt, preferred_element_type=F32)
        m = jnp.maximum(jnp.max(s, axis=-1, keepdims=True), s_self)
        p = jnp.exp2(s - m)
        p_self = jnp.exp2(s_self - m)
        l = jnp.sum(p, axis=-1, keepdims=True) + p_self
        o = (lax.dot_general(p.astype(BF16), vt, NT_DIMS, preferred_element_type=F32) + p_self * vn_ref[0, 0]) / l
        out = jnp.where(row == a, o, out)
    o_ref[0, 0] = out


def _moba_attend(cache_kt, cache_vt, pages, q_grp, k_new, v_new):
    b, _, npg = pages.shape
    gsz = A_HEADS // A_KV_HEADS

    def pspec(a, i):
        return pl.BlockSpec((1, HEAD_DIM, PAGE_SIZE), lambda bb, kv, pg: (pg[bb, kv * gsz + a, i], kv, 0))

    blk = lambda r: pl.BlockSpec((1, 1, r, HEAD_DIM), lambda bb, kv, pg: (bb, kv, 0, 0))
    page_specs = [pspec(a, i) for a in range(gsz) for i in range(npg)]
    grid_spec = pltpu.PrefetchScalarGridSpec(
        num_scalar_prefetch=1, grid=(b, A_KV_HEADS),
        in_specs=[blk(q_grp.shape[2]), blk(1), blk(1)] + page_specs * 2,
        out_specs=blk(q_grp.shape[2]))
    return pl.pallas_call(
        functools.partial(_moba_attend_body, npg=npg, gsz=gsz), grid_spec=grid_spec,
        out_shape=jax.ShapeDtypeStruct(q_grp.shape, F32),
        compiler_params=_params("parallel", "parallel"),
        name="moba_decode_attend",
    )(pages, q_grp, k_new, v_new, *([cache_kt] * (gsz * npg)), *([cache_vt] * (gsz * npg)))


def _fox_decode_body(pt_ref, q_ref, kn_ref, vn_ref, lfn_ref, k_hbm, v_hbm, f_hbm, o_ref,
                     kbuf, vbuf, fbuf, ksem, vsem, fsem, *, cp, n_pages):
    q = q_ref[0]

    qb = q.astype(BF16)
    r0 = lax.broadcasted_iota(jnp.int32, (PAGE_SIZE, 2 * PAGE_SIZE), 0)
    r1 = lax.broadcasted_iota(jnp.int32, (PAGE_SIZE, 2 * PAGE_SIZE), 1)
    tri = ((r0 > r1) | (r1 >= PAGE_SIZE)).astype(BF16)
    state = [jnp.sum(q * kn_ref[0], axis=-1, keepdims=True), jnp.ones((B_HEADS, 1), F32),
             jnp.broadcast_to(vn_ref[0], (B_HEADS, LANES)), lfn_ref[0]]

    def compute(c, slot):
        m, l, acc, run = state
        d = sum(jnp.dot(part, tri, preferred_element_type=F32) for part in _split3(fbuf[slot]))
        biases = []
        for i in range(cp):
            di = d[i * B_HEADS:(i + 1) * B_HEADS]
            biases.append(run + di[:, :PAGE_SIZE])
            run = run + di[:, PAGE_SIZE:PAGE_SIZE + 1]
        kt = kbuf[slot].astype(BF16)
        vt = vbuf[slot].astype(BF16)
        s = jnp.dot(qb, kt, preferred_element_type=F32) + jnp.concatenate(biases, axis=1) * LOG2E
        m_new = jnp.maximum(m, jnp.max(s, axis=-1, keepdims=True))
        p = jnp.exp2(s - m_new)
        alpha = jnp.exp2(m - m_new)
        state[:] = [m_new, alpha * l + jnp.sum(p, axis=-1, keepdims=True),
                    alpha * acc + lax.dot_general(p.astype(BF16), vt, NT_DIMS, preferred_element_type=F32), run]

    lanes = lambda buf, i: buf.at[:, pl.ds(i * PAGE_SIZE, PAGE_SIZE)]
    inputs = [_PagedInput(k_hbm, kbuf, ksem, lanes), _PagedInput(v_hbm, vbuf, vsem, lanes),
              _PagedInput(f_hbm, fbuf, fsem, lambda buf, i: buf.at[pl.ds(i * B_HEADS, B_HEADS), :])]
    _paged_chunks(pt_ref, inputs, n_pages, cp, lambda p: n_pages - 1 - p, compute)
    o_ref[0] = state[2] / state[1]


def _fox_decode(cache_kt, cache_vt, cache_ft, page_table, qbd, k_new, v_new, lf_new):
    b, n_pages = page_table.shape
    cp = _pages_per_step(n_pages)
    assert n_pages % cp == 0
    vec = lambda n: pl.BlockSpec((1, 1, n), lambda bb, pt: (bb, 0, 0))
    hbm = pl.BlockSpec(memory_space=pl.ANY)
    grid_spec = pltpu.PrefetchScalarGridSpec(
        num_scalar_prefetch=1, grid=(b,),
        in_specs=[pl.BlockSpec((1, B_HEADS, LANES), lambda bb, pt: (bb, 0, 0)), vec(LANES), vec(LANES),
                  pl.BlockSpec((1, B_HEADS, 1), lambda bb, pt: (bb, 0, 0)), hbm, hbm, hbm],
        out_specs=pl.BlockSpec((1, B_HEADS, LANES), lambda bb, pt: (bb, 0, 0)),
        scratch_shapes=[pltpu.VMEM((2, LANES, cp * PAGE_SIZE), F32), pltpu.VMEM((2, LANES, cp * PAGE_SIZE), F32),
                        pltpu.VMEM((2, cp * B_HEADS, PAGE_SIZE), F32),
                        pltpu.SemaphoreType.DMA((2,)), pltpu.SemaphoreType.DMA((2,)), pltpu.SemaphoreType.DMA((2,))])
    return pl.pallas_call(
        functools.partial(_fox_decode_body, cp=cp, n_pages=n_pages), grid_spec=grid_spec,
        out_shape=jax.ShapeDtypeStruct((b, B_HEADS, LANES), F32),
        compiler_params=_params("arbitrary"),
        name="fox_decode",
    )(page_table, qbd, k_new[:, None, :], v_new[:, None, :], lf_new[:, :, None], cache_kt, cache_vt, cache_ft)


def _mla_decode_body(pt_ref, ql_ref, qp_ref, cn_ref, pn_ref, c_hbm, p_hbm, o_ref, cbuf, pbuf, csem, psem, *, cp, n_pages):
    ql, qp = ql_ref[0], qp_ref[0]
    qlb, qpb = ql.astype(BF16), qp.astype(BF16)
    state = [jnp.sum(ql * cn_ref[0], axis=-1, keepdims=True) + jnp.sum(qp * pn_ref[0], axis=-1, keepdims=True),
             jnp.ones((C_HEADS, 1), F32), jnp.broadcast_to(cn_ref[0], (C_HEADS, C_KV_LORA))]

    def compute(c, slot):
        m, l, acc = state
        ck = cbuf[slot].astype(BF16)
        pt = pbuf[slot].astype(BF16)
        s = (lax.dot_general(qlb, ck, NT_DIMS, preferred_element_type=F32)
             + jnp.dot(qpb, pt, preferred_element_type=F32))
        m_new = jnp.maximum(m, jnp.max(s, axis=-1, keepdims=True))
        p = jnp.exp2(s - m_new)
        alpha = jnp.exp2(m - m_new)
        state[:] = [m_new, alpha * l + jnp.sum(p, axis=-1, keepdims=True),
                    alpha * acc + jnp.dot(p.astype(BF16), ck, preferred_element_type=F32)]

    inputs = [_PagedInput(c_hbm, cbuf, csem, lambda buf, i: buf.at[pl.ds(i * PAGE_SIZE, PAGE_SIZE), :]),
              _PagedInput(p_hbm, pbuf, psem, lambda buf, i: buf.at[:, pl.ds(i * PAGE_SIZE, PAGE_SIZE)])]
    _paged_chunks(pt_ref, inputs, n_pages, cp, lambda p: p, compute)
    o_ref[0] = state[2] / state[1]


def _mla_decode(cache_c, cache_pt, page_table, q_lat, q_pe, c_new, p_new):
    b, n_pages = page_table.shape
    cp = _pages_per_step(n_pages)
    assert n_pages % cp == 0
    blk = lambda r, n: pl.BlockSpec((1, r, n), lambda bb, pt: (bb, 0, 0))
    hbm = pl.BlockSpec(memory_space=pl.ANY)
    grid_spec = pltpu.PrefetchScalarGridSpec(
        num_scalar_prefetch=1, grid=(b,),
        in_specs=[blk(C_HEADS, C_KV_LORA), blk(C_HEADS, C_ROPE), blk(1, C_KV_LORA), blk(1, C_ROPE), hbm, hbm],
        out_specs=blk(C_HEADS, C_KV_LORA),
        scratch_shapes=[pltpu.VMEM((2, cp * PAGE_SIZE, C_KV_LORA), F32), pltpu.VMEM((2, C_ROPE, cp * PAGE_SIZE), F32),
                        pltpu.SemaphoreType.DMA((2,)), pltpu.SemaphoreType.DMA((2,))])
    return pl.pallas_call(
        functools.partial(_mla_decode_body, cp=cp, n_pages=n_pages), grid_spec=grid_spec,
        out_shape=jax.ShapeDtypeStruct((b, C_HEADS, C_KV_LORA), F32),
        compiler_params=_params("arbitrary"),
        name="mla_decode",
    )(page_table, q_lat, q_pe, c_new[:, None, :], p_new[:, None, :], cache_c, cache_pt)


def _feature_major(cache):
    pool, page = cache.shape[:2]
    return cache.transpose(0, 2, 3, 1).reshape(pool, -1, page)


def _block_diag_q(q):
    b = q.shape[0]
    qh = q.astype(F32).reshape(b, 2, 4, HEAD_DIM)
    z = jnp.zeros_like(qh)
    return jnp.concatenate([jnp.concatenate([qh[:, 0], z[:, 0]], axis=-1),
                            jnp.concatenate([z[:, 1], qh[:, 1]], axis=-1)], axis=1)


def _diag_halves(o):
    b = o.shape[0]
    return jnp.concatenate([o[:, :4, :HEAD_DIM].reshape(b, 4 * HEAD_DIM), o[:, 4:, HEAD_DIM:].reshape(b, 4 * HEAD_DIM)], axis=-1)


def kernel(x_prompt, x_sample, cache_moba_k, cache_moba_v, cache_fox_k, cache_fox_v, cache_fox_logf, cache_mla_ckv,
           cache_mla_kpe, state_ffn_conv, page_table, norm_mix, norm_ffn, norm_final, w_in_even, b_forget, w_out_even,
           w_dq, g_q, w_uq, w_dkv, g_kv, w_uk, w_uv, w_out_odd, w_gate, w_up, conv_w, conv_b, w_down):
    bp, sp, d = x_prompt.shape
    bs, ts, _ = x_sample.shape
    assert bp == 1 and ts == 1
    n_pages = page_table.shape[1]
    past = n_pages * PAGE_SIZE
    assert past % MOBA_BLOCK == 0 and past // MOBA_BLOCK >= MOBA_TOPK and sp % MOBA_BLOCK == 0
    depth = norm_mix.shape[0]
    pos_p = jnp.arange(sp, dtype=jnp.int32)
    pos_s = jnp.full((bs,), past, jnp.int32)
    xp = x_prompt[0]
    xs = x_sample[:, 0]
    outs_p = {k: [] for k in ("mk", "mv", "fk", "fv", "fl", "ck", "kp", "cv")}
    outs_s = {k: [] for k in ("mk", "mv", "fk", "fv", "fl", "ck", "kp", "cv")}
    row2 = lambda v: v.reshape(1, -1)

    for layer in range(depth):
        if layer % 2 == 0:
            e = layer // 2
            w = w_in_even[e]
            w_cat = jnp.concatenate([w, jnp.zeros((d, LANES - B_HEADS), F32)], axis=1).astype(BF16)
            bf_pad = jnp.concatenate([b_forget[e], jnp.zeros((LANES - B_HEADS,), F32)]).reshape(1, LANES)
            g = row2(norm_mix[layer])
            w_out = w_out_even[e].astype(BF16)
            qa, ka, va, qb, kb, vb, lf, chi, cmid, clo = _even_proj(xp, g, w_cat, bf_pad, pos_p)
            nb = sp // MOBA_BLOCK
            bias = _moba_gate(qa, _block_mean(ka))
            qpa = jnp.concatenate([qa.reshape(sp, A_HEADS, HEAD_DIM), bias.reshape(sp, A_HEADS, HEAD_DIM)], axis=-1)
            onehot = (pos_p[:, None] // MOBA_BLOCK == jnp.arange(HEAD_DIM)[None, :]).astype(BF16)
            kpa = jnp.concatenate([ka.astype(BF16).reshape(sp, 2, HEAD_DIM).transpose(1, 0, 2),
                                   jnp.broadcast_to(onehot[None], (2, sp, HEAD_DIM))], axis=-1)
            oa = _flash(qpa.reshape(sp, A_HEADS * LANES), kpa, va.astype(BF16).T, g=4, tq=512, tk=512)
            cparts = jnp.stack([chi, cmid, clo], axis=-1)
            sel = jnp.tile(jnp.repeat(jnp.eye(4, dtype=BF16), 3, axis=1), (2, 1))
            aug_q = jnp.concatenate([jnp.broadcast_to(sel[None], (sp, B_HEADS, 12)), cparts,
                                     jnp.zeros((sp, B_HEADS, HEAD_DIM - 15), BF16)], axis=-1)
            qpb = jnp.concatenate([qb.reshape(sp, B_HEADS, HEAD_DIM), aug_q], axis=-1)
            aug_k = jnp.concatenate([(-cparts).reshape(sp, 2, 12), jnp.ones((sp, 2, 3), BF16),
                                     jnp.zeros((sp, 2, HEAD_DIM - 15), BF16)], axis=-1)
            kpb = jnp.concatenate([kb.astype(BF16).reshape(sp, 2, HEAD_DIM), aug_k], axis=-1).transpose(1, 0, 2)
            ob = _flash(qpb.reshape(sp, B_HEADS * LANES), kpb, vb.astype(BF16).T, g=4, tq=512, tk=512)
            xp = _out_proj(xp, [oa, ob], [w_out[:A_HEADS * HEAD_DIM], w_out[A_HEADS * HEAD_DIM:]])
            for key, val in zip(("mk", "mv", "fk", "fv"), (ka, va, kb, vb)):
                outs_p[key].append(val.reshape(1, sp, 2, HEAD_DIM))
            outs_p["fl"].append(lf.reshape(1, sp, B_HEADS))
            qa, ka, va, qb, kb, vb, lf, _, _, _ = _even_proj(xs, g, w_cat, bf_pad, pos_s)
            ck, cv = _feature_major(cache_moba_k[e]), _feature_major(cache_moba_v[e])
            qbd = _block_diag_q(qa)
            picks = _moba_pick(ck, page_table, qbd)[:, :, :MOBA_TOPK]
            per = MOBA_BLOCK // PAGE_SIZE
            logical = (picks[..., None] * per + jnp.arange(per, dtype=jnp.int32)).reshape(bs, A_HEADS, MOBA_TOPK * per)
            pages = jnp.take_along_axis(page_table[:, None, :], logical, axis=2)
            gsz = A_HEADS // A_KV_HEADS
            q_grp = qa.astype(F32).reshape(bs, A_KV_HEADS, gsz, HEAD_DIM)
            q_grp = jnp.concatenate([q_grp, jnp.zeros((bs, A_KV_HEADS, 8 - gsz, HEAD_DIM), F32)], axis=2)
            oa = _moba_attend(ck, cv, pages, q_grp, ka.reshape(bs, A_KV_HEADS, 1, HEAD_DIM),
                              va.reshape(bs, A_KV_HEADS, 1, HEAD_DIM))
            oa = oa[:, :, :gsz].reshape(bs, A_HEADS * HEAD_DIM).astype(BF16)
            ob = _fox_decode(_feature_major(cache_fox_k[e]), _feature_major(cache_fox_v[e]),
                             cache_fox_logf[e].transpose(0, 2, 1), page_table, _block_diag_q(qb), kb, vb, lf)
            ob = _diag_halves(ob).astype(BF16)
            xs = _out_proj(xs, [oa, ob], [w_out[:A_HEADS * HEAD_DIM], w_out[A_HEADS * HEAD_DIM:]])
            for key, val in zip(("mk", "mv", "fk", "fv"), (ka, va, kb, vb)):
                outs_s[key].append(val.reshape(bs, 1, 2, HEAD_DIM))
            outs_s["fl"].append(lf.reshape(bs, 1, B_HEADS))
        else:
            o = layer // 2
            dq = C_NOPE + C_ROPE
            wq = w_uq[o].reshape(C_Q_LORA, C_HEADS, dq)
            wuq_slots = jnp.concatenate([wq, jnp.zeros((C_Q_LORA, C_HEADS, LANES - dq), F32)], axis=-1)
            wuq_slots = wuq_slots.reshape(C_Q_LORA, C_HEADS * LANES).astype(BF16)
            wdkv_pad = jnp.concatenate([w_dkv[o], jnp.zeros((d, LANES - C_ROPE), F32)], axis=1).astype(BF16)
            wk_top = jnp.concatenate([w_uk[o], jnp.zeros((C_KV_LORA, C_HEADS, LANES - C_NOPE), F32)], axis=-1)
            place = jnp.concatenate([jnp.zeros((C_ROPE, C_NOPE), F32), jnp.eye(C_ROPE, dtype=F32),
                                     jnp.zeros((C_ROPE, LANES - dq), F32)], axis=1)
            wk_mid = jnp.broadcast_to(place[:, None, :], (C_ROPE, C_HEADS, LANES))
            wk_aug = jnp.concatenate([wk_top, wk_mid, jnp.zeros((LANES - C_ROPE, C_HEADS, LANES), F32)], axis=0)
            wk_aug = wk_aug.reshape(C_KV_LORA + LANES, C_HEADS * LANES).astype(BF16)
            wv = w_uv[o].reshape(C_KV_LORA, C_HEADS * C_V).astype(BF16)
            ws = (row2(norm_mix[layer]), w_dq[o].astype(BF16), row2(g_q[o]), wuq_slots, wdkv_pad, row2(g_kv[o]), wk_aug, wv)
            w_out = w_out_odd[o].astype(BF16)
            qp, ckv, kpe, kp, v = _mla_proj(xp, *ws, pos_p)
            oc = _flash(qp, kp.reshape(sp, C_HEADS, LANES).transpose(1, 0, 2), v.T, g=1, tq=1024, tk=512)
            xp = _out_proj(xp, [oc], [w_out])
            outs_p["ck"].append(ckv.reshape(1, sp, C_KV_LORA))
            outs_p["kp"].append(kpe.reshape(1, sp, C_ROPE))
            qp, ckv, kpe, _, _ = _mla_proj(xs, *ws, pos_s)
            qh = qp.reshape(bs, C_HEADS, LANES).transpose(1, 0, 2)
            q_lat = _head_mm(qh[:, :, :C_NOPE], w_uk[o].transpose(1, 2, 0).astype(BF16))
            o_lat = _mla_decode(cache_mla_ckv[o], cache_mla_kpe[o].transpose(0, 2, 1), page_table, q_lat.transpose(1, 0, 2),
                                qh[:, :, C_NOPE:dq].transpose(1, 0, 2).astype(F32), ckv, kpe)
            oc = _head_mm(o_lat.transpose(1, 0, 2), w_uv[o].transpose(1, 0, 2).astype(BF16))
            xs = _out_proj(xs, [oc.transpose(1, 0, 2).reshape(bs, C_HEADS * C_V).astype(BF16)], [w_out])
            outs_s["ck"].append(ckv.reshape(bs, 1, C_KV_LORA))
            outs_s["kp"].append(kpe.reshape(bs, 1, C_ROPE))
        ffn_w = (row2(norm_ffn[layer]), w_gate[layer].astype(BF16), w_up[layer].astype(BF16), conv_w[layer],
                 row2(conv_b[layer]), w_down[layer].astype(BF16))
        xp, st = _ffn(xp, *ffn_w)
        outs_p["cv"].append(st[8 - (CONV_W - 1):].reshape(1, CONV_W - 1, -1))
        xs, gate_rows = _ffn(xs, *ffn_w, state=state_ffn_conv[layer])
        outs_s["cv"].append(jnp.stack([state_ffn_conv[layer][:, 1], gate_rows], axis=1))

    gf = row2(norm_final)
    order = ("mk", "mv", "fk", "fv", "fl", "ck", "kp", "cv")
    return ((_final_norm(xp, gf).reshape(bp, sp, d), _final_norm(xs, gf).reshape(bs, ts, d))
            + tuple(jnp.stack(outs_p[k]) for k in order) + tuple(jnp.stack(outs_s[k]) for k in order))
```

```python
import functools
import math

import numpy as np
import jax
import jax.numpy as jnp
from jax import lax
from jax.experimental import pallas as pl
from jax.experimental.pallas import tpu as pltpu

F32 = jnp.float32
BF16 = jnp.bfloat16

HEAD_DIM = 64
A_HEADS, A_KV_HEADS = 8, 2
B_HEADS, B_KV_HEADS = 8, 2
C_HEADS, C_Q_LORA, C_KV_LORA, C_NOPE, C_ROPE, C_V = 16, 512, 256, 64, 32, 64
MOBA_BLOCK, MOBA_TOPK = 256, 3
PAGE_SIZE = 128
CONV_W = 3
ROPE_THETA = 10000.0
EPS = 1e-6
LANES = 128
NEG = -1e30
LOG2E = math.log2(math.e)
VMEM_LIMIT = 56 * 1024 * 1024
HIGHEST = lax.Precision.HIGHEST
NT_DIMS = (((1,), (1,)), ((), ()))
TN_DIMS = (((0,), (0,)), ((), ()))


def _params(*sem):
    return pltpu.CompilerParams(dimension_semantics=sem, vmem_limit_bytes=VMEM_LIMIT)


def _rms(x, g):
    return x * lax.rsqrt(jnp.mean(x * x, axis=-1, keepdims=True) + EPS) * g


def _rope_tables(pos, kind, freq, half, d):
    inv = ROPE_THETA ** (-jnp.arange(half, dtype=F32) * (2.0 / d))
    ang = pos.astype(F32)[:, None] * inv[None, :]
    cos, sin = jnp.cos(ang)[:, freq], jnp.sin(ang)[:, freq]
    kind = jnp.asarray(kind)[None, :]
    return (jnp.where(kind == 0, 1.0, cos).astype(F32), jnp.where(kind == 1, -sin, 0.0).astype(F32),
            jnp.where(kind == 2, sin, 0.0).astype(F32))


def _rope_apply(z, cos, sa, sb, half):
    outs = []
    for c in range(z.shape[1] // LANES):
        x = z[:, c * LANES:(c + 1) * LANES]
        outs.append(x * cos + pltpu.roll(x, LANES - half, 1) * sa + pltpu.roll(x, half, 1) * sb)
    return outs[0] if len(outs) == 1 else jnp.concatenate(outs, axis=1)


def _split3(x):
    hi = x.astype(BF16)
    r = x - hi.astype(F32)
    mid = r.astype(BF16)
    lo = (r - mid.astype(F32)).astype(BF16)
    return hi, mid, lo


def _row_tile(t, want):
    return want if t % want == 0 else t


def _even_proj_body(x_ref, g_ref, w_ref, bf_ref, cos_ref, sa_ref, sb_ref,
                    qa_ref, ka_ref, va_ref, qb_ref, kb_ref, vb_ref, lf_ref, chi_ref, cmid_ref, clo_ref, carry_ref):
    @pl.when(pl.program_id(0) == 0)
    def _():
        carry_ref[...] = jnp.zeros_like(carry_ref)

    tm = x_ref.shape[0]
    h = _rms(x_ref[...], g_ref[...]).astype(BF16)
    z = jnp.dot(h, w_ref[...], preferred_element_type=F32)
    cos, sa, sb = cos_ref[...], sa_ref[...], sb_ref[...]
    half = HEAD_DIM // 2
    scale = HEAD_DIM ** -0.5 * LOG2E
    o = 0
    qa_ref[...] = (_rope_apply(z[:, o:o + 512], cos, sa, sb, half) * scale).astype(BF16); o += 512
    ka_ref[...] = _rope_apply(z[:, o:o + 128], cos, sa, sb, half); o += 128
    va_ref[...] = z[:, o:o + 128]; o += 128
    qb_ref[...] = (z[:, o:o + 512] * scale).astype(BF16); o += 512
    kb_ref[...] = z[:, o:o + 128]; o += 128
    vb_ref[...] = z[:, o:o + 128]; o += 128
    fl = z[:, o:o + 128] + bf_ref[...]
    lf = jnp.minimum(fl, 0.0) - jnp.log1p(jnp.exp(-jnp.abs(fl)))
    lf_ref[...] = lf[:, :B_HEADS]
    tri = (lax.broadcasted_iota(jnp.int32, (tm, tm), 0) >= lax.broadcasted_iota(jnp.int32, (tm, tm), 1)).astype(F32)
    cum = jnp.dot(tri, lf, preferred_element_type=F32, precision=HIGHEST) + carry_ref[...]
    carry_ref[...] = cum[tm - 1:tm, :]
    hi, mid, lo = _split3(cum[:, :B_HEADS] * LOG2E)
    chi_ref[...] = hi
    cmid_ref[...] = mid
    clo_ref[...] = lo


def _even_proj(x, g, w_cat, bf_pad, pos):
    t, d = x.shape
    tm = _row_tile(t, 512)
    lane = np.arange(LANES) % HEAD_DIM
    cos, sa, sb = _rope_tables(pos, np.where(lane < 32, 1, 2), lane % 32, HEAD_DIM // 2, HEAD_DIM)
    row = lambda n: pl.BlockSpec((tm, n), lambda i: (i, 0))
    full = lambda a: pl.BlockSpec(a.shape, lambda i: (0, 0))
    outs = [(512, BF16), (128, F32), (128, F32), (512, BF16), (128, F32), (128, F32), (B_HEADS, F32),
            (B_HEADS, BF16), (B_HEADS, BF16), (B_HEADS, BF16)]
    return pl.pallas_call(
        _even_proj_body,
        grid=(t // tm,),
        in_specs=[row(d), full(g), full(w_cat), full(bf_pad), row(LANES), row(LANES), row(LANES)],
        out_specs=[row(n) for n, _ in outs],
        out_shape=[jax.ShapeDtypeStruct((t, n), dt) for n, dt in outs],
        scratch_shapes=[pltpu.VMEM((1, LANES), F32)],
        compiler_params=_params("arbitrary"),
        name="even_proj",
    )(x, g, w_cat, bf_pad, cos, sa, sb)


def _block_mean_body(k_ref, o_ref):
    nb = o_ref.shape[0]
    o_ref[...] = jnp.mean(k_ref[...].reshape(nb, MOBA_BLOCK, LANES), axis=1)


def _block_mean(k):
    t = k.shape[0]
    nb = t // MOBA_BLOCK
    return pl.pallas_call(
        _block_mean_body,
        out_shape=jax.ShapeDtypeStruct((nb, LANES), F32),
        compiler_params=_params(),
        name="moba_block_mean",
    )(k)


def _top3_bias(gate, n_valid, lane):
    cand = jnp.where(lane < n_valid, gate, -jnp.inf)
    keep = lane >= n_valid
    for _ in range(MOBA_TOPK):
        mx = jnp.max(cand, axis=-1, keepdims=True)
        first = jnp.min(jnp.where(cand == mx, lane, gate.shape[-1]), axis=-1, keepdims=True)
        hit = (lane == first) & (mx > -jnp.inf)
        keep = keep | hit
        cand = jnp.where(hit, -jnp.inf, cand)
    return jnp.where(keep, 0.0, NEG)


def _moba_gate_body(q_ref, km_ref, o_ref):
    tm = q_ref.shape[0]
    nb = km_ref.shape[0]
    g = A_HEADS // A_KV_HEADS
    pos = pl.program_id(0) * tm + lax.broadcasted_iota(jnp.int32, (tm, 1), 0)
    own = pos // MOBA_BLOCK
    lane = lax.broadcasted_iota(jnp.int32, (tm, nb), 1)
    for h in range(A_HEADS):
        q = q_ref[:, h * HEAD_DIM:(h + 1) * HEAD_DIM].astype(F32)
        km = km_ref[:, (h // g) * HEAD_DIM:(h // g + 1) * HEAD_DIM]
        gate = lax.dot_general(q, km, NT_DIMS, preferred_element_type=F32, precision=HIGHEST)
        bias = _top3_bias(gate, own, lane)
        if nb < HEAD_DIM:
            bias = jnp.concatenate([bias, jnp.zeros((tm, HEAD_DIM - nb), F32)], axis=1)
        o_ref[:, h * HEAD_DIM:(h + 1) * HEAD_DIM] = bias.astype(BF16)


def _moba_gate(q, kmean):
    t = q.shape[0]
    tm = _row_tile(t, 512)
    assert kmean.shape[0] <= HEAD_DIM, "one bias lane per key block"
    return pl.pallas_call(
        _moba_gate_body,
        grid=(t // tm,),
        in_specs=[pl.BlockSpec((tm, q.shape[1]), lambda i: (i, 0)), pl.BlockSpec(kmean.shape, lambda i: (0, 0))],
        out_specs=pl.BlockSpec((tm, A_HEADS * HEAD_DIM), lambda i: (i, 0)),
        out_shape=jax.ShapeDtypeStruct((t, A_HEADS * HEAD_DIM), BF16),
        compiler_params=_params("parallel"),
        name="moba_gate",
    )(q, kmean)


FLASH_COLS = 512


def _flash_body(q_ref, k_ref, vt_ref, o_ref, *, g, tq, tk, tpi):
    i = pl.program_id(1)
    r = g * tq
    n_full = (i * tq) // tk
    n_mask = tq // tk
    ncb = r // FLASH_COLS
    qs = [jnp.concatenate([q_ref[:, (s * g + a) * LANES:(s * g + a + 1) * LANES] for a in range(g)], axis=0)
          if g > 1 else q_ref[:, s * LANES:(s + 1) * LANES] for s in range(2)]

    def tiles(j0, carry, masked, nt):
        state = list(carry)
        chains = [(u, s, cb) for u in range(nt) for s in range(2) for cb in range(ncb)]
        starts = [pl.multiple_of((j0 + u) * tk, tk) for u in range(nt)]
        scores, probs = {}, {}

        def qk(c):
            u, s, cb = chains[c]
            k = k_ref[s, pl.ds(starts[u], tk), :]
            q = qs[s][cb * FLASH_COLS:(cb + 1) * FLASH_COLS]
            scores[c] = lax.dot_general(k, q, NT_DIMS, preferred_element_type=F32)

        def softmax(c):
            u, s, cb = chains[c]
            m, l, acc = state[s * ncb + cb]
            x = scores.pop(c)
            if masked:
                kpos = starts[u] + lax.broadcasted_iota(jnp.int32, x.shape, 0)
                qpos = i * tq + (cb * FLASH_COLS + lax.broadcasted_iota(jnp.int32, x.shape, 1)) % tq
                x = jnp.where(kpos <= qpos, x, NEG)
            m_new = jnp.maximum(m, jnp.max(x, axis=0, keepdims=True))
            p = jnp.exp2(x - m_new)
            alpha = jnp.exp2(m - m_new)
            probs[c] = (alpha, p.astype(BF16))
            state[s * ncb + cb] = (m_new, alpha * l + jnp.sum(p, axis=0, keepdims=True), acc)

        def pv(c):
            u, s, cb = chains[c]
            alpha, p = probs.pop(c)
            m, l, acc = state[s * ncb + cb]
            vt = vt_ref[s * HEAD_DIM:(s + 1) * HEAD_DIM, pl.ds(starts[u], tk)]
            state[s * ncb + cb] = (m, l, alpha * acc + jnp.dot(vt, p, preferred_element_type=F32))

        n = len(chains)
        for t in range(n + 2):
            if t < n:
                qk(t)
            if 0 <= t - 1 < n:
                softmax(t - 1)
            if 0 <= t - 2 < n:
                pv(t - 2)
        return tuple(state)

    init = tuple((jnp.full((1, FLASH_COLS), NEG, F32), jnp.zeros((1, FLASH_COLS), F32),
                  jnp.zeros((HEAD_DIM, FLASH_COLS), F32)) for _ in range(2 * ncb))
    carry = lax.fori_loop(0, n_full // tpi, lambda j, c: tiles(j * tpi, c, False, tpi), init)
    carry = tiles(n_full, carry, True, n_mask)
    for s in range(2):
        o = jnp.concatenate([carry[s * ncb + cb][2] / carry[s * ncb + cb][1] for cb in range(ncb)], axis=1)
        for a in range(g):
            col = (s * g + a) * HEAD_DIM
            o_ref[:, col:col + HEAD_DIM] = o[:, a * tq:(a + 1) * tq].T.astype(BF16)


def _flash(qp, kp, vt, *, g, tq, tk):
    t = qp.shape[0]
    nc = kp.shape[0] // 2
    tq, tk = min(tq, t), min(tk, t)
    tpi = tq // tk
    assert t % tq == 0 and tq % tk == 0 and (g * tq) % FLASH_COLS == 0
    return pl.pallas_call(
        functools.partial(_flash_body, g=g, tq=tq, tk=tk, tpi=tpi),
        grid=(nc, t // tq),
        in_specs=[pl.BlockSpec((tq, 2 * g * LANES), lambda c, i: (i, c)),
                  pl.BlockSpec((2, t, LANES), lambda c, i: (c, 0, 0)),
                  pl.BlockSpec((LANES, t), lambda c, i: (c, 0))],
        out_specs=pl.BlockSpec((tq, 2 * g * HEAD_DIM), lambda c, i: (i, c)),
        out_shape=jax.ShapeDtypeStruct((t, nc * 2 * g * HEAD_DIM), BF16),
        compiler_params=_params("parallel", "arbitrary"),
        name="flash_attention",
    )(qp, kp, vt)


def _out_proj_body(*refs, n):
    x_ref, o_ref = refs[0], refs[-1]
    acc = x_ref[...]
    for a_ref, w_ref in zip(refs[1:1 + n], refs[1 + n:1 + 2 * n]):
        acc = acc + jnp.dot(a_ref[...], w_ref[...], preferred_element_type=F32)
    o_ref[...] = acc


def _out_proj(x, acts, ws):
    t, d = x.shape
    tm = _row_tile(t, 512)
    return pl.pallas_call(
        functools.partial(_out_proj_body, n=len(acts)),
        grid=(t // tm,),
        in_specs=[pl.BlockSpec((tm, d), lambda i: (i, 0))]
                 + [pl.BlockSpec((tm, a.shape[1]), lambda i: (i, 0)) for a in acts]
                 + [pl.BlockSpec(w.shape, lambda i: (0, 0)) for w in ws],
        out_specs=pl.BlockSpec((tm, d), lambda i: (i, 0)),
        out_shape=jax.ShapeDtypeStruct((t, d), F32),
        compiler_params=_params("parallel"),
        name="out_proj",
    )(x, *acts, *ws)


FF_CHUNK = 256


def _ffn_body(*refs, seq):
    if seq:
        x_ref, gn_ref, wg_ref, wu_ref, cw_ref, cb_ref, wd_ref, y_ref, st_ref, carry_ref = refs
    else:
        x_ref, gn_ref, wg_ref, wu_ref, cw_ref, cb_ref, wd_ref, s0_ref, s1_ref, y_ref, st_ref = refs
    tm = x_ref.shape[0]
    ff = wg_ref.shape[1]
    if seq:
        @pl.when(pl.program_id(0) == 0)
        def _():
            carry_ref[...] = jnp.zeros_like(carry_ref)
        row = lax.broadcasted_iota(jnp.int32, (tm, FF_CHUNK), 0)

    x = x_ref[...]
    h = _rms(x, gn_ref[...]).astype(BF16)
    acc = x
    for c in range(ff // FF_CHUNK):
        sl = slice(c * FF_CHUNK, (c + 1) * FF_CHUNK)
        gch = jnp.dot(h, wg_ref[:, sl], preferred_element_type=F32)
        uch = jnp.dot(h, wu_ref[:, sl], preferred_element_type=F32)
        if seq:
            prev = carry_ref[:, sl]
            g1 = jnp.where(row == 0, prev[7:8, :], pltpu.roll(gch, 1, 0))
            g2 = jnp.where(row == 0, prev[6:7, :], jnp.where(row == 1, prev[7:8, :], pltpu.roll(gch, 2, 0)))
            carry_ref[:, sl] = gch[tm - 8:tm, :]
            st_ref[:, sl] = gch[tm - 8:tm, :]
        else:
            g1, g2 = s1_ref[:, sl], s0_ref[:, sl]
            st_ref[:, sl] = gch
        gc = cb_ref[:, sl] + cw_ref[0:1, sl] * g2 + cw_ref[1:2, sl] * g1 + cw_ref[2:3, sl] * gch
        act = gc / (1.0 + jnp.exp(-gc)) * uch
        acc = acc + jnp.dot(act.astype(BF16), wd_ref[sl, :], preferred_element_type=F32)
    y_ref[...] = acc


def _ffn(x, gn, wg, wu, cw, cb, wd, state=None):
    t, d = x.shape
    ff = wg.shape[1]
    assert ff % FF_CHUNK == 0
    seq = state is None
    tm = _row_tile(t, 512) if seq else t
    assert tm >= 8
    row = lambda n: pl.BlockSpec((tm, n), lambda i: (i, 0))
    full = lambda a: pl.BlockSpec(a.shape, lambda i: (0, 0))
    ins = [x, gn, wg, wu, cw, cb, wd]
    in_specs = [row(d)] + [full(a) for a in ins[1:]]
    if seq:
        st_shape, st_spec = (8, ff), pl.BlockSpec((8, ff), lambda i: (0, 0))
        scratch = [pltpu.VMEM((8, ff), F32)]
    else:
        ins += [state[:, 0], state[:, 1]]
        in_specs += [row(ff), row(ff)]
        st_shape, st_spec = (t, ff), row(ff)
        scratch = []
    return pl.pallas_call(
        functools.partial(_ffn_body, seq=seq),
        grid=(t // tm,),
        in_specs=in_specs,
        out_specs=[row(d), st_spec],
        out_shape=[jax.ShapeDtypeStruct((t, d), F32), jax.ShapeDtypeStruct(st_shape, F32)],
        scratch_shapes=scratch,
        compiler_params=_params("arbitrary"),
        name="conv_ffn",
    )(*ins)


def _mla_proj_body(x_ref, g_ref, wdq_ref, gq_ref, wuq_ref, wdkv_ref, gkv_ref, wk_ref, wv_ref,
                   cq_ref, sq_ref, tq_ref, ck_ref, sk_ref, tk_ref,
                   qp_ref, ckv_ref, kpe_ref, kp_ref, v_ref):
    h = _rms(x_ref[...], g_ref[...]).astype(BF16)
    cq = _rms(jnp.dot(h, wdq_ref[...], preferred_element_type=F32), gq_ref[...]).astype(BF16)
    q = jnp.dot(cq, wuq_ref[...], preferred_element_type=F32)
    scale = (C_NOPE + C_ROPE) ** -0.5 * LOG2E
    qp_ref[...] = (_rope_apply(q, cq_ref[...], sq_ref[...], tq_ref[...], C_ROPE // 2) * scale).astype(BF16)
    kv = jnp.dot(h, wdkv_ref[...], preferred_element_type=F32)
    ckv = _rms(kv[:, :C_KV_LORA], gkv_ref[...])
    kpe = _rope_apply(kv[:, C_KV_LORA:], ck_ref[...], sk_ref[...], tk_ref[...], C_ROPE // 2)
    ckv_ref[...] = ckv
    kpe_ref[...] = kpe[:, :C_ROPE]
    kin = jnp.concatenate([ckv, kpe], axis=1).astype(BF16)
    kp_ref[...] = jnp.dot(kin, wk_ref[...], preferred_element_type=F32).astype(BF16)
    v_ref[...] = jnp.dot(kin[:, :C_KV_LORA], wv_ref[...], preferred_element_type=F32).astype(BF16)


def _mla_proj(x, g, wdq, gq, wuq_slots, wdkv_pad, gkv, wk_aug, wv, pos):
    t, d = x.shape
    tm = _row_tile(t, 512)
    lane = np.arange(LANES)
    in_q = (lane >= C_NOPE) & (lane < C_NOPE + C_ROPE)
    dq = lane - C_NOPE
    tabs_q = _rope_tables(pos, np.where(in_q, np.where(dq < C_ROPE // 2, 1, 2), 0),
                          np.where(in_q, dq % (C_ROPE // 2), 0), C_ROPE // 2, C_ROPE)
    in_k = lane < C_ROPE
    tabs_k = _rope_tables(pos, np.where(in_k, np.where(lane < C_ROPE // 2, 1, 2), 0),
                          np.where(in_k, lane % (C_ROPE // 2), 0), C_ROPE // 2, C_ROPE)
    row = lambda n: pl.BlockSpec((tm, n), lambda i: (i, 0))
    full = lambda a: pl.BlockSpec(a.shape, lambda i: (0, 0))
    ws = [g, wdq, gq, wuq_slots, wdkv_pad, gkv, wk_aug, wv]
    outs = [(C_HEADS * LANES, BF16), (C_KV_LORA, F32), (C_ROPE, F32), (C_HEADS * LANES, BF16), (C_HEADS * C_V, BF16)]
    return pl.pallas_call(
        _mla_proj_body,
        grid=(t // tm,),
        in_specs=[row(d)] + [full(a) for a in ws] + [row(LANES)] * 6,
        out_specs=[row(n) for n, _ in outs],
        out_shape=[jax.ShapeDtypeStruct((t, n), dt) for n, dt in outs],
        compiler_params=_params("parallel"),
        name="mla_proj",
    )(x, *ws, *tabs_q, *tabs_k)


def _head_mm_body(x_ref, w_ref, o_ref):
    o_ref[0] = jnp.dot(x_ref[0].astype(BF16), w_ref[0], preferred_element_type=F32)


def _head_mm(x, w):
    hh, b, k = x.shape
    n = w.shape[2]
    return pl.pallas_call(
        _head_mm_body,
        grid=(hh,),
        in_specs=[pl.BlockSpec((1, b, k), lambda i: (i, 0, 0)), pl.BlockSpec((1, k, n), lambda i: (i, 0, 0))],
        out_specs=pl.BlockSpec((1, b, n), lambda i: (i, 0, 0)),
        out_shape=jax.ShapeDtypeStruct((hh, b, n), F32),
        compiler_params=_params("parallel"),
        name="head_matmul",
    )(x, w)


def _norm_body(x_ref, g_ref, o_ref):
    o_ref[...] = _rms(x_ref[...], g_ref[...])


def _final_norm(x, g):
    t, d = x.shape
    tm = _row_tile(t, 1024)
    return pl.pallas_call(
        _norm_body,
        grid=(t // tm,),
        in_specs=[pl.BlockSpec((tm, d), lambda i: (i, 0)), pl.BlockSpec(g.shape, lambda i: (0, 0))],
        out_specs=pl.BlockSpec((tm, d), lambda i: (i, 0)),
        out_shape=jax.ShapeDtypeStruct((t, d), F32),
        compiler_params=_params("parallel"),
        name="final_norm",
    )(x, g)


class _PagedInput:
    def __init__(self, hbm, buf, sem, window):
        self.hbm, self.buf, self.sem, self.window = hbm, buf, sem, window

    def copy(self, page, slot, i):
        return pltpu.make_async_copy(self.hbm.at[page], self.window(self.buf.at[slot], i), self.sem.at[slot])


def _paged_chunks(pt_ref, inputs, n_pages, cp, page_of, compute):
    b = pl.program_id(0)
    n_chunks = n_pages // cp
    ahead = PAGE_BUFFERS - 1
    assert n_chunks % PAGE_BUFFERS == 0, "chunk c uses buffer c % PAGE_BUFFERS, so every sequence must start on buffer 0"

    def chunk_copies(bb, c):
        slot = c % PAGE_BUFFERS
        return [inp.copy(pt_ref[bb, page_of(c * cp + i)], slot, i) for i in range(cp) for inp in inputs]

    @pl.when(b == 0)
    def _():
        for c in range(ahead):
            for cpy in chunk_copies(0, c):
                cpy.start()

    for c in range(n_chunks):
        nxt = c + ahead
        if nxt < n_chunks:
            for cpy in chunk_copies(b, nxt):
                cpy.start()
        else:
            @pl.when(b + 1 < pl.num_programs(0))
            def _(nxt=nxt):
                for cpy in chunk_copies(b + 1, nxt - n_chunks):
                    cpy.start()
        for cpy in chunk_copies(b, c):
            cpy.wait()
        compute(c, c % PAGE_BUFFERS)


PAGE_BUFFERS = 4


def _pages_per_step(n_pages):
    return min(16, n_pages // PAGE_BUFFERS)


def _top3_ids(cand, lane, out_lane):
    res = jnp.zeros(out_lane.shape, jnp.int32)
    for r in range(MOBA_TOPK):
        mx = jnp.max(cand, axis=-1, keepdims=True)
        first = jnp.min(jnp.where(cand == mx, lane, cand.shape[-1]), axis=-1, keepdims=True)
        res = jnp.where(out_lane == r, first, res)
        cand = jnp.where(lane == first, -jnp.inf, cand)
    return res


def _moba_pick_body(pt_ref, q_ref, k_hbm, o_ref, kbuf, ksem, *, cp, n_pages):
    per = MOBA_BLOCK // PAGE_SIZE
    qb = q_ref[0].astype(BF16)
    lane = lax.broadcasted_iota(jnp.int32, (A_HEADS, LANES), 1)
    state = [jnp.full((A_HEADS, LANES), -jnp.inf, F32)]

    def compute(c, slot):
        s = jnp.dot(qb, kbuf[slot].astype(BF16), preferred_element_type=F32)
        gate = state[0]
        for n in range(cp // per):
            val = jnp.sum(s[:, n * MOBA_BLOCK:(n + 1) * MOBA_BLOCK], axis=-1, keepdims=True) * (1.0 / MOBA_BLOCK)
            gate = jnp.where(lane == c * (cp // per) + n, val, gate)
        state[0] = gate

    inputs = [_PagedInput(k_hbm, kbuf, ksem, lambda buf, i: buf.at[:, pl.ds(i * PAGE_SIZE, PAGE_SIZE)])]
    _paged_chunks(pt_ref, inputs, n_pages, cp, lambda p: p, compute)
    o_ref[0] = _top3_ids(state[0], lane, lane)


def _moba_pick(cache_kt, page_table, qbd):
    b, n_pages = page_table.shape
    cp = _pages_per_step(n_pages)
    per = MOBA_BLOCK // PAGE_SIZE
    assert n_pages % cp == 0 and cp % per == 0 and n_pages // per <= LANES
    grid_spec = pltpu.PrefetchScalarGridSpec(
        num_scalar_prefetch=1, grid=(b,),
        in_specs=[pl.BlockSpec((1, A_HEADS, LANES), lambda bb, pt: (bb, 0, 0)), pl.BlockSpec(memory_space=pl.ANY)],
        out_specs=pl.BlockSpec((1, A_HEADS, LANES), lambda bb, pt: (bb, 0, 0)),
        scratch_shapes=[pltpu.VMEM((PAGE_BUFFERS,LANES, cp * PAGE_SIZE), F32), pltpu.SemaphoreType.DMA((PAGE_BUFFERS,))])
    return pl.pallas_call(
        functools.partial(_moba_pick_body, cp=cp, n_pages=n_pages), grid_spec=grid_spec,
        out_shape=jax.ShapeDtypeStruct((b, A_HEADS, LANES), jnp.int32),
        compiler_params=_params("arbitrary"),
        name="moba_decode_pick",
    )(page_table, qbd, cache_kt)


def _moba_attend_body(pg_ref, q_ref, kn_ref, vn_ref, *refs, npg, gsz):
    n = gsz * npg
    k_refs, v_refs, o_ref = refs[:n], refs[n:2 * n], refs[2 * n]
    q = q_ref[0, 0]
    qb = q.astype(BF16)
    s_self = jnp.sum(q * kn_ref[0, 0], axis=-1, keepdims=True)
    row = lax.broadcasted_iota(jnp.int32, (q.shape[0], HEAD_DIM), 0)
    out = jnp.zeros((q.shape[0], HEAD_DIM), F32)
    for a in range(gsz):
        kt = jnp.concatenate([r[0].astype(BF16) for r in k_refs[a * npg:(a + 1) * npg]], axis=1)
        vt = jnp.concatenate([r[0].astype(BF16) for r in v_refs[a * npg:(a + 1) * npg]], axis=1)
        s = jnp.dot(qb, kt, preferred_element_type=F32)
        m = jnp.maximum(jnp.max(s, axis=-1, keepdims=True), s_self)
        p = jnp.exp2(s - m)
        p_self = jnp.exp2(s_self - m)
        l = jnp.sum(p, axis=-1, keepdims=True) + p_self
        o = (lax.dot_general(p.astype(BF16), vt, NT_DIMS, preferred_element_type=F32) + p_self * vn_ref[0, 0]) / l
        out = jnp.where(row == a, o, out)
    o_ref[0, 0] = out


def _moba_attend(cache_kt, cache_vt, pages, q_grp, k_new, v_new):
    b, _, npg = pages.shape
    gsz = A_HEADS // A_KV_HEADS

    def pspec(a, i):
        return pl.BlockSpec((1, HEAD_DIM, PAGE_SIZE), lambda bb, kv, pg: (pg[bb, kv * gsz + a, i], kv, 0))

    blk = lambda r: pl.BlockSpec((1, 1, r, HEAD_DIM), lambda bb, kv, pg: (bb, kv, 0, 0))
    page_specs = [pspec(a, i) for a in range(gsz) for i in range(npg)]
    grid_spec = pltpu.PrefetchScalarGridSpec(
        num_scalar_prefetch=1, grid=(b, A_KV_HEADS),
        in_specs=[blk(q_grp.shape[2]), blk(1), blk(1)] + page_specs * 2,
        out_specs=blk(q_grp.shape[2]))
    return pl.pallas_call(
        functools.partial(_moba_attend_body, npg=npg, gsz=gsz), grid_spec=grid_spec,
        out_shape=jax.ShapeDtypeStruct(q_grp.shape, F32),
        compiler_params=_params("parallel", "parallel"),
        name="moba_decode_attend",
    )(pages, q_grp, k_new, v_new, *([cache_kt] * (gsz * npg)), *([cache_vt] * (gsz * npg)))


def _fox_decode_body(pt_ref, q_ref, kn_ref, vn_ref, lfn_ref, k_hbm, v_hbm, f_hbm, o_ref,
                     kbuf, vbuf, fbuf, ksem, vsem, fsem, *, cp, n_pages):
    q = q_ref[0]

    qb = q.astype(BF16)
    r0 = lax.broadcasted_iota(jnp.int32, (PAGE_SIZE, 2 * PAGE_SIZE), 0)
    r1 = lax.broadcasted_iota(jnp.int32, (PAGE_SIZE, 2 * PAGE_SIZE), 1)
    tri = ((r0 > r1) | (r1 >= PAGE_SIZE)).astype(BF16)
    state = [jnp.sum(q * kn_ref[0], axis=-1, keepdims=True), jnp.ones((B_HEADS, 1), F32),
             jnp.broadcast_to(vn_ref[0], (B_HEADS, LANES)), lfn_ref[0]]

    def compute(c, slot):
        m, l, acc, run = state
        d = sum(jnp.dot(part, tri, preferred_element_type=F32) for part in _split3(fbuf[slot]))
        biases = []
        for i in range(cp):
            di = d[i * B_HEADS:(i + 1) * B_HEADS]
            biases.append(run + di[:, :PAGE_SIZE])
            run = run + di[:, PAGE_SIZE:PAGE_SIZE + 1]
        kt = kbuf[slot].astype(BF16)
        vt = vbuf[slot].astype(BF16)
        s = jnp.dot(qb, kt, preferred_element_type=F32) + jnp.concatenate(biases, axis=1) * LOG2E
        m_new = jnp.maximum(m, jnp.max(s, axis=-1, keepdims=True))
        p = jnp.exp2(s - m_new)
        alpha = jnp.exp2(m - m_new)
        state[:] = [m_new, alpha * l + jnp.sum(p, axis=-1, keepdims=True),
                    alpha * acc + lax.dot_general(p.astype(BF16), vt, NT_DIMS, preferred_element_type=F32), run]

    lanes = lambda buf, i: buf.at[:, pl.ds(i * PAGE_SIZE, PAGE_SIZE)]
    inputs = [_PagedInput(k_hbm, kbuf, ksem, lanes), _PagedInput(v_hbm, vbuf, vsem, lanes),
              _PagedInput(f_hbm, fbuf, fsem, lambda buf, i: buf.at[pl.ds(i * B_HEADS, B_HEADS), :])]
    _paged_chunks(pt_ref, inputs, n_pages, cp, lambda p: n_pages - 1 - p, compute)
    o_ref[0] = state[2] / state[1]


def _fox_decode(cache_kt, cache_vt, cache_ft, page_table, qbd, k_new, v_new, lf_new):
    b, n_pages = page_table.shape
    cp = _pages_per_step(n_pages)
    assert n_pages % cp == 0
    vec = lambda n: pl.BlockSpec((1, 1, n), lambda bb, pt: (bb, 0, 0))
    hbm = pl.BlockSpec(memory_space=pl.ANY)
    grid_spec = pltpu.PrefetchScalarGridSpec(
        num_scalar_prefetch=1, grid=(b,),
        in_specs=[pl.BlockSpec((1, B_HEADS, LANES), lambda bb, pt: (bb, 0, 0)), vec(LANES), vec(LANES),
                  pl.BlockSpec((1, B_HEADS, 1), lambda bb, pt: (bb, 0, 0)), hbm, hbm, hbm],
        out_specs=pl.BlockSpec((1, B_HEADS, LANES), lambda bb, pt: (bb, 0, 0)),
        scratch_shapes=[pltpu.VMEM((PAGE_BUFFERS,LANES, cp * PAGE_SIZE), F32), pltpu.VMEM((PAGE_BUFFERS,LANES, cp * PAGE_SIZE), F32),
                        pltpu.VMEM((PAGE_BUFFERS,cp * B_HEADS, PAGE_SIZE), F32),
                        pltpu.SemaphoreType.DMA((PAGE_BUFFERS,)), pltpu.SemaphoreType.DMA((PAGE_BUFFERS,)), pltpu.SemaphoreType.DMA((PAGE_BUFFERS,))])
    return pl.pallas_call(
        functools.partial(_fox_decode_body, cp=cp, n_pages=n_pages), grid_spec=grid_spec,
        out_shape=jax.ShapeDtypeStruct((b, B_HEADS, LANES), F32),
        compiler_params=_params("arbitrary"),
        name="fox_decode",
    )(page_table, qbd, k_new[:, None, :], v_new[:, None, :], lf_new[:, :, None], cache_kt, cache_vt, cache_ft)


def _mla_decode_body(pt_ref, ql_ref, qp_ref, cn_ref, pn_ref, c_hbm, p_hbm, o_ref, cbuf, pbuf, csem, psem, *, cp, n_pages):
    ql, qp = ql_ref[0], qp_ref[0]
    qlb, qpb = ql.astype(BF16), qp.astype(BF16)
    state = [jnp.sum(ql * cn_ref[0], axis=-1, keepdims=True) + jnp.sum(qp * pn_ref[0], axis=-1, keepdims=True),
             jnp.ones((C_HEADS, 1), F32), jnp.broadcast_to(cn_ref[0], (C_HEADS, C_KV_LORA))]

    def compute(c, slot):
        m, l, acc = state
        ck = cbuf[slot].astype(BF16)
        pt = pbuf[slot].astype(BF16)
        s = (lax.dot_general(qlb, ck, NT_DIMS, preferred_element_type=F32)
             + jnp.dot(qpb, pt, preferred_element_type=F32))
        m_new = jnp.maximum(m, jnp.max(s, axis=-1, keepdims=True))
        p = jnp.exp2(s - m_new)
        alpha = jnp.exp2(m - m_new)
        state[:] = [m_new, alpha * l + jnp.sum(p, axis=-1, keepdims=True),
                    alpha * acc + jnp.dot(p.astype(BF16), ck, preferred_element_type=F32)]

    inputs = [_PagedInput(c_hbm, cbuf, csem, lambda buf, i: buf.at[pl.ds(i * PAGE_SIZE, PAGE_SIZE), :]),
              _PagedInput(p_hbm, pbuf, psem, lambda buf, i: buf.at[:, pl.ds(i * PAGE_SIZE, PAGE_SIZE)])]
    _paged_chunks(pt_ref, inputs, n_pages, cp, lambda p: p, compute)
    o_ref[0] = state[2] / state[1]


def _mla_decode(cache_c, cache_pt, page_table, q_lat, q_pe, c_new, p_new):
    b, n_pages = page_table.shape
    cp = _pages_per_step(n_pages)
    assert n_pages % cp == 0
    blk = lambda r, n: pl.BlockSpec((1, r, n), lambda bb, pt: (bb, 0, 0))
    hbm = pl.BlockSpec(memory_space=pl.ANY)
    grid_spec = pltpu.PrefetchScalarGridSpec(
        num_scalar_prefetch=1, grid=(b,),
        in_specs=[blk(C_HEADS, C_KV_LORA), blk(C_HEADS, C_ROPE), blk(1, C_KV_LORA), blk(1, C_ROPE), hbm, hbm],
        out_specs=blk(C_HEADS, C_KV_LORA),
        scratch_shapes=[pltpu.VMEM((PAGE_BUFFERS,cp * PAGE_SIZE, C_KV_LORA), F32), pltpu.VMEM((PAGE_BUFFERS,C_ROPE, cp * PAGE_SIZE), F32),
                        pltpu.SemaphoreType.DMA((PAGE_BUFFERS,)), pltpu.SemaphoreType.DMA((PAGE_BUFFERS,))])
    return pl.pallas_call(
        functools.partial(_mla_decode_body, cp=cp, n_pages=n_pages), grid_spec=grid_spec,
        out_shape=jax.ShapeDtypeStruct((b, C_HEADS, C_KV_LORA), F32),
        compiler_params=_params("arbitrary"),
        name="mla_decode",
    )(page_table, q_lat, q_pe, c_new[:, None, :], p_new[:, None, :], cache_c, cache_pt)


def _feature_major(cache):
    pool, page = cache.shape[:2]
    return cache.transpose(0, 2, 3, 1).reshape(pool, -1, page)


def _block_diag_q(q):
    b = q.shape[0]
    qh = q.astype(F32).reshape(b, 2, 4, HEAD_DIM)
    z = jnp.zeros_like(qh)
    return jnp.concatenate([jnp.concatenate([qh[:, 0], z[:, 0]], axis=-1),
                            jnp.concatenate([z[:, 1], qh[:, 1]], axis=-1)], axis=1)


def _diag_halves(o):
    b = o.shape[0]
    return jnp.concatenate([o[:, :4, :HEAD_DIM].reshape(b, 4 * HEAD_DIM), o[:, 4:, HEAD_DIM:].reshape(b, 4 * HEAD_DIM)], axis=-1)


def kernel(x_prompt, x_sample, cache_moba_k, cache_moba_v, cache_fox_k, cache_fox_v, cache_fox_logf, cache_mla_ckv,
           cache_mla_kpe, state_ffn_conv, page_table, norm_mix, norm_ffn, norm_final, w_in_even, b_forget, w_out_even,
           w_dq, g_q, w_uq, w_dkv, g_kv, w_uk, w_uv, w_out_odd, w_gate, w_up, conv_w, conv_b, w_down):
    bp, sp, d = x_prompt.shape
    bs, ts, _ = x_sample.shape
    assert bp == 1 and ts == 1
    n_pages = page_table.shape[1]
    past = n_pages * PAGE_SIZE
    assert past % MOBA_BLOCK == 0 and past // MOBA_BLOCK >= MOBA_TOPK and sp % MOBA_BLOCK == 0
    depth = norm_mix.shape[0]
    pos_p = jnp.arange(sp, dtype=jnp.int32)
    pos_s = jnp.full((bs,), past, jnp.int32)
    xp = x_prompt[0]
    xs = x_sample[:, 0]
    outs_p = {k: [] for k in ("mk", "mv", "fk", "fv", "fl", "ck", "kp", "cv")}
    outs_s = {k: [] for k in ("mk", "mv", "fk", "fv", "fl", "ck", "kp", "cv")}
    row2 = lambda v: v.reshape(1, -1)

    for layer in range(depth):
        if layer % 2 == 0:
            e = layer // 2
            w = w_in_even[e]
            w_cat = jnp.concatenate([w, jnp.zeros((d, LANES - B_HEADS), F32)], axis=1).astype(BF16)
            bf_pad = jnp.concatenate([b_forget[e], jnp.zeros((LANES - B_HEADS,), F32)]).reshape(1, LANES)
            g = row2(norm_mix[layer])
            w_out = w_out_even[e].astype(BF16)
            qa, ka, va, qb, kb, vb, lf, chi, cmid, clo = _even_proj(xp, g, w_cat, bf_pad, pos_p)
            nb = sp // MOBA_BLOCK
            bias = _moba_gate(qa, _block_mean(ka))
            qpa = jnp.concatenate([qa.reshape(sp, A_HEADS, HEAD_DIM), bias.reshape(sp, A_HEADS, HEAD_DIM)], axis=-1)
            onehot = (pos_p[:, None] // MOBA_BLOCK == jnp.arange(HEAD_DIM)[None, :]).astype(BF16)
            kpa = jnp.concatenate([ka.astype(BF16).reshape(sp, 2, HEAD_DIM).transpose(1, 0, 2),
                                   jnp.broadcast_to(onehot[None], (2, sp, HEAD_DIM))], axis=-1)
            oa = _flash(qpa.reshape(sp, A_HEADS * LANES), kpa, va.astype(BF16).T, g=4, tq=512, tk=512)
            cparts = jnp.stack([chi, cmid, clo], axis=-1)
            sel = jnp.tile(jnp.repeat(jnp.eye(4, dtype=BF16), 3, axis=1), (2, 1))
            aug_q = jnp.concatenate([jnp.broadcast_to(sel[None], (sp, B_HEADS, 12)), cparts,
                                     jnp.zeros((sp, B_HEADS, HEAD_DIM - 15), BF16)], axis=-1)
            qpb = jnp.concatenate([qb.reshape(sp, B_HEADS, HEAD_DIM), aug_q], axis=-1)
            aug_k = jnp.concatenate([(-cparts).reshape(sp, 2, 12), jnp.ones((sp, 2, 3), BF16),
                                     jnp.zeros((sp, 2, HEAD_DIM - 15), BF16)], axis=-1)
            kpb = jnp.concatenate([kb.astype(BF16).reshape(sp, 2, HEAD_DIM), aug_k], axis=-1).transpose(1, 0, 2)
            ob = _flash(qpb.reshape(sp, B_HEADS * LANES), kpb, vb.astype(BF16).T, g=4, tq=512, tk=512)
            xp = _out_proj(xp, [oa, ob], [w_out[:A_HEADS * HEAD_DIM], w_out[A_HEADS * HEAD_DIM:]])
            for key, val in zip(("mk", "mv", "fk", "fv"), (ka, va, kb, vb)):
                outs_p[key].append(val.reshape(1, sp, 2, HEAD_DIM))
            outs_p["fl"].append(lf.reshape(1, sp, B_HEADS))
            qa, ka, va, qb, kb, vb, lf, _, _, _ = _even_proj(xs, g, w_cat, bf_pad, pos_s)
            ck, cv = _feature_major(cache_moba_k[e]), _feature_major(cache_moba_v[e])
            qbd = _block_diag_q(qa)
            picks = _moba_pick(ck, page_table, qbd)[:, :, :MOBA_TOPK]
            per = MOBA_BLOCK // PAGE_SIZE
            logical = (picks[..., None] * per + jnp.arange(per, dtype=jnp.int32)).reshape(bs, A_HEADS, MOBA_TOPK * per)
            pages = jnp.take_along_axis(page_table[:, None, :], logical, axis=2)
            gsz = A_HEADS // A_KV_HEADS
            q_grp = qa.astype(F32).reshape(bs, A_KV_HEADS, gsz, HEAD_DIM)
            q_grp = jnp.concatenate([q_grp, jnp.zeros((bs, A_KV_HEADS, 8 - gsz, HEAD_DIM), F32)], axis=2)
            oa = _moba_attend(ck, cv, pages, q_grp, ka.reshape(bs, A_KV_HEADS, 1, HEAD_DIM),
                              va.reshape(bs, A_KV_HEADS, 1, HEAD_DIM))
            oa = oa[:, :, :gsz].reshape(bs, A_HEADS * HEAD_DIM).astype(BF16)
            ob = _fox_decode(_feature_major(cache_fox_k[e]), _feature_major(cache_fox_v[e]),
                             cache_fox_logf[e].transpose(0, 2, 1), page_table, _block_diag_q(qb), kb, vb, lf)
            ob = _diag_halves(ob).astype(BF16)
            xs = _out_proj(xs, [oa, ob], [w_out[:A_HEADS * HEAD_DIM], w_out[A_HEADS * HEAD_DIM:]])
            for key, val in zip(("mk", "mv", "fk", "fv"), (ka, va, kb, vb)):
                outs_s[key].append(val.reshape(bs, 1, 2, HEAD_DIM))
            outs_s["fl"].append(lf.reshape(bs, 1, B_HEADS))
        else:
            o = layer // 2
            dq = C_NOPE + C_ROPE
            wq = w_uq[o].reshape(C_Q_LORA, C_HEADS, dq)
            wuq_slots = jnp.concatenate([wq, jnp.zeros((C_Q_LORA, C_HEADS, LANES - dq), F32)], axis=-1)
            wuq_slots = wuq_slots.reshape(C_Q_LORA, C_HEADS * LANES).astype(BF16)
            wdkv_pad = jnp.concatenate([w_dkv[o], jnp.zeros((d, LANES - C_ROPE), F32)], axis=1).astype(BF16)
            wk_top = jnp.concatenate([w_uk[o], jnp.zeros((C_KV_LORA, C_HEADS, LANES - C_NOPE), F32)], axis=-1)
            place = jnp.concatenate([jnp.zeros((C_ROPE, C_NOPE), F32), jnp.eye(C_ROPE, dtype=F32),
                                     jnp.zeros((C_ROPE, LANES - dq), F32)], axis=1)
            wk_mid = jnp.broadcast_to(place[:, None, :], (C_ROPE, C_HEADS, LANES))
            wk_aug = jnp.concatenate([wk_top, wk_mid, jnp.zeros((LANES - C_ROPE, C_HEADS, LANES), F32)], axis=0)
            wk_aug = wk_aug.reshape(C_KV_LORA + LANES, C_HEADS * LANES).astype(BF16)
            wv = w_uv[o].reshape(C_KV_LORA, C_HEADS * C_V).astype(BF16)
            ws = (row2(norm_mix[layer]), w_dq[o].astype(BF16), row2(g_q[o]), wuq_slots, wdkv_pad, row2(g_kv[o]), wk_aug, wv)
            w_out = w_out_odd[o].astype(BF16)
            qp, ckv, kpe, kp, v = _mla_proj(xp, *ws, pos_p)
            oc = _flash(qp, kp.reshape(sp, C_HEADS, LANES).transpose(1, 0, 2), v.T, g=1, tq=1024, tk=512)
            xp = _out_proj(xp, [oc], [w_out])
            outs_p["ck"].append(ckv.reshape(1, sp, C_KV_LORA))
            outs_p["kp"].append(kpe.reshape(1, sp, C_ROPE))
            qp, ckv, kpe, _, _ = _mla_proj(xs, *ws, pos_s)
            qh = qp.reshape(bs, C_HEADS, LANES).transpose(1, 0, 2)
            q_lat = _head_mm(qh[:, :, :C_NOPE], w_uk[o].transpose(1, 2, 0).astype(BF16))
            o_lat = _mla_decode(cache_mla_ckv[o], cache_mla_kpe[o].transpose(0, 2, 1), page_table, q_lat.transpose(1, 0, 2),
                                qh[:, :, C_NOPE:dq].transpose(1, 0, 2).astype(F32), ckv, kpe)
            oc = _head_mm(o_lat.transpose(1, 0, 2), w_uv[o].transpose(1, 0, 2).astype(BF16))
            xs = _out_proj(xs, [oc.transpose(1, 0, 2).reshape(bs, C_HEADS * C_V).astype(BF16)], [w_out])
            outs_s["ck"].append(ckv.reshape(bs, 1, C_KV_LORA))
            outs_s["kp"].append(kpe.reshape(bs, 1, C_ROPE))
        ffn_w = (row2(norm_ffn[layer]), w_gate[layer].astype(BF16), w_up[layer].astype(BF16), conv_w[layer],
                 row2(conv_b[layer]), w_down[layer].astype(BF16))
        xp, st = _ffn(xp, *ffn_w)
        outs_p["cv"].append(st[8 - (CONV_W - 1):].reshape(1, CONV_W - 1, -1))
        xs, gate_rows = _ffn(xs, *ffn_w, state=state_ffn_conv[layer])
        outs_s["cv"].append(jnp.stack([state_ffn_conv[layer][:, 1], gate_rows], axis=1))

    gf = row2(norm_final)
    order = ("mk", "mv", "fk", "fv", "fl", "ck", "kp", "cv")
    return ((_final_norm(xp, gf).reshape(bp, sp, d), _final_norm(xs, gf).reshape(bs, ts, d))
            + tuple(jnp.stack(outs_p[k]) for k in order) + tuple(jnp.stack(outs_s[k]) for k in order))
```
